```python
import math
import jax, jax.numpy as jnp
from jax import lax
import numpy as np

D_MODEL = 1024
BATCH = 8
SEQ = 4096
DEPTH = 1
DEC_BATCH = 128
DEC_SEQ = 4
PAST_LEN = 8192
PAGE_SIZE = 128

N_HEADS = 8
HEAD_DIM = 64
V_DIM = 2 * HEAD_DIM
QK_WIDTH = N_HEADS * 2 * HEAD_DIM
ATTN_WIDTH = N_HEADS * V_DIM
POOL_WINDOWS = (2, 4, 8, 16)
POOL_GROUPS = len(POOL_WINDOWS)
POOL_WIDTH = D_MODEL
POOL_GROUP_DIM = POOL_WIDTH // POOL_GROUPS
POOL_BUF = max(POOL_WINDOWS) - 1
NUM_BUCKETS = 32
MAX_EXACT = NUM_BUCKETS // 2
MAX_DISTANCE = 128
Q_BLOCK = 128
RMS_EPS = 1e-6
NEG_INF = -1e30
ATTN_SCALE = HEAD_DIM ** -0.5
IN_WIDTH = 2 * POOL_WIDTH + 2 * QK_WIDTH + 2 * ATTN_WIDTH + 2 * D_MODEL

kernel_name = 'diff_pool_hybrid_step'


def rmsnorm(x, g):
    xf = x.astype(jnp.float32)
    y = xf * lax.rsqrt(jnp.mean(xf * xf, axis=-1, keepdims=True) + RMS_EPS)
    return (y * g.astype(jnp.float32)).astype(x.dtype)


def rel_bucket(n):
    n = jnp.maximum(n, 0)
    nf = jnp.maximum(n, 1).astype(jnp.float32)
    large = MAX_EXACT + (jnp.log(nf / MAX_EXACT) / math.log(MAX_DISTANCE / MAX_EXACT)
                         * (NUM_BUCKETS - MAX_EXACT)).astype(jnp.int32)
    large = jnp.minimum(large, NUM_BUCKETS - 1)
    return jnp.where(n < MAX_EXACT, n, large)


def rel_bias_lookup(rel_bias, n):
    return jnp.moveaxis(rel_bias[rel_bucket(n)].astype(jnp.float32), -1, 0)


def adaln(c, w_ada, b_ada):
    mod = jax.nn.silu(c) @ w_ada + b_ada
    shift, scale, gate = jnp.split(mod, 3, axis=-1)
    return shift[:, None, :], scale[:, None, :], gate[:, None, :]


def layer_inputs(x, c, w_ada, b_ada, g_pre, w_in):
    shift, scale, gate = adaln(c, w_ada, b_ada)
    h = rmsnorm(x, g_pre) * (1 + scale) + shift
    z = h @ w_in
    widths = (POOL_WIDTH, POOL_WIDTH, QK_WIDTH, QK_WIDTH, ATTN_WIDTH, ATTN_WIDTH, D_MODEL)
    idx = list(np.cumsum(widths))
    u, pg, q, k, v, ag, mgp, mga = jnp.split(z, idx, axis=-1)
    B, T = x.shape[:2]
    q = q.reshape(B, T, N_HEADS, 2 * HEAD_DIM)
    k = k.reshape(B, T, N_HEADS, 2 * HEAD_DIM)
    v = v.reshape(B, T, N_HEADS, V_DIM)
    return gate, u, pg, q, k, v, ag, mgp, mga


def pool_branch(u_ext, gate, start_pos, w_grp, pool_scale, w_proj):
    B, L, _ = u_ext.shape
    T = L - POOL_BUF
    cs = jnp.cumsum(u_ext.astype(jnp.float32), axis=1)
    cs = jnp.concatenate([jnp.zeros((B, 1, POOL_WIDTH), jnp.float32), cs], axis=1)
    end = cs[:, POOL_BUF + 1:]
    pos = start_pos + jnp.arange(T)
    means = []
    for g, w in enumerate(POOL_WINDOWS):
        sl = slice(g * POOL_GROUP_DIM, (g + 1) * POOL_GROUP_DIM)
        begin = cs[:, POOL_BUF + 1 - w:POOL_BUF + 1 - w + T, sl]
        cnt = jnp.minimum(pos + 1, w).astype(jnp.float32)[None, :, None]
        means.append((end[..., sl] - begin) / cnt)
    u_new = u_ext[:, POOL_BUF:]
    d = (jnp.concatenate(means, axis=-1) - u_new.astype(jnp.float32)).astype(u_ext.dtype)
    dg = d.reshape(B, T, POOL_GROUPS, POOL_GROUP_DIM)
    y = jnp.einsum('btgc,gcd->btgd', dg, w_grp).reshape(B, T, POOL_WIDTH) * pool_scale
    return (y * jax.nn.silu(gate)) @ w_proj


def diff_lambda(lq1, lk1, lq2, lk2, lam_init):
    return (jnp.exp(jnp.sum(lq1.astype(jnp.float32) * lk1.astype(jnp.float32)))
            - jnp.exp(jnp.sum(lq2.astype(jnp.float32) * lk2.astype(jnp.float32))) + lam_init)


def prompt_attention(q, k, v, rel_bias, lam):
    B = q.shape[0]
    q1, q2 = q[..., :HEAD_DIM], q[..., HEAD_DIM:]
    k1, k2 = k[..., :HEAD_DIM], k[..., HEAD_DIM:]
    k_pos = jnp.arange(SEQ)

    def block(i):
        qs = i * Q_BLOCK
        q1b = lax.dynamic_slice_in_dim(q1, qs, Q_BLOCK, axis=1)
        q2b = lax.dynamic_slice_in_dim(q2, qs, Q_BLOCK, axis=1)
        n = (qs + jnp.arange(Q_BLOCK))[:, None] - k_pos[None, :]
        bias = rel_bias_lookup(rel_bias, n)
        mask = n >= 0

        def probs(qb, kk):
            s = jnp.einsum('bqhd,bkhd->bhqk', qb, kk).astype(jnp.float32) * ATTN_SCALE + bias
            return jax.nn.softmax(jnp.where(mask, s, NEG_INF), axis=-1)

        a = probs(q1b, k1) - lam * probs(q2b, k2)
        return jnp.einsum('bhqk,bkhe->bqhe', a.astype(v.dtype), v)

    o = lax.map(block, jnp.arange(SEQ // Q_BLOCK))
    return o.transpose(1, 0, 2, 3, 4).reshape(B, SEQ, N_HEADS, V_DIM)


def online_update(state, q, kk, vv, bias, mask):
    q1, q2 = q[..., :HEAD_DIM], q[..., HEAD_DIM:]
    k1, k2 = kk[..., :HEAD_DIM], kk[..., HEAD_DIM:]
    vf = vv.astype(jnp.float32)

    def one(m, l, acc, qq, kq):
        s = jnp.einsum('bqhd,bkhd->bhqk', qq, kq).astype(jnp.float32) * ATTN_SCALE + bias
        s = jnp.where(mask, s, NEG_INF)
        m_new = jnp.maximum(m, jnp.max(s, axis=-1))
        p = jnp.exp(s - m_new[..., None])
        corr = jnp.exp(m - m_new)
        l_new = l * corr + jnp.sum(p, axis=-1)
        acc_new = acc * corr[..., None] + jnp.einsum('bhqk,bkhe->bhqe', p, vf)
        return m_new, l_new, acc_new

    m1, l1, a1, m2, l2, a2 = state
    m1, l1, a1 = one(m1, l1, a1, q1, k1)
    m2, l2, a2 = one(m2, l2, a2, q2, k2)
    return (m1, l1, a1, m2, l2, a2)


def sample_attention(q, k_new, v_new, cache_k, cache_v, layer, page_table, rel_bias, lam):
    Bd, T = q.shape[:2]
    n_pages = page_table.shape[1]
    q_pos = PAST_LEN + jnp.arange(T)
    m0 = jnp.full((Bd, N_HEADS, T), NEG_INF, jnp.float32)
    l0 = jnp.zeros((Bd, N_HEADS, T), jnp.float32)
    a0 = jnp.zeros((Bd, N_HEADS, T, V_DIM), jnp.float32)
    init = (m0, l0, a0, m0, l0, a0)

    def page_step(state, p):
        pid = page_table[:, p]
        kp = cache_k[layer, pid]
        vp = cache_v[layer, pid]
        n = q_pos[:, None] - (p * PAGE_SIZE + jnp.arange(PAGE_SIZE))[None, :]
        return online_update(state, q, kp, vp, rel_bias_lookup(rel_bias, n), n >= 0), None

    state, _ = lax.scan(page_step, init, jnp.arange(n_pages))
    n = q_pos[:, None] - q_pos[None, :]
    m1, l1, a1, m2, l2, a2 = online_update(state, q, k_new, v_new,
                                           rel_bias_lookup(rel_bias, n), n >= 0)
    o = a1 / l1[..., None] - lam * (a2 / l2[..., None])
    return o.transpose(0, 2, 1, 3).astype(v_new.dtype)


def attn_finish(o, gate, g_subln, lam_init, w_proj):
    B, T = o.shape[:2]
    o = rmsnorm(o, g_subln) * (1.0 - lam_init)
    return (o.reshape(B, T, ATTN_WIDTH) * jax.nn.silu(gate)) @ w_proj


def layer_output(x, y_pool, y_attn, mgp, mga, w_out, g_post, gate):
    merged = jax.nn.sigmoid(mgp) * y_pool + jax.nn.sigmoid(mga) * y_attn
    return x + gate * rmsnorm(merged @ w_out, g_post)


def setup_inputs(seed: int = 0) -> dict:
    key = jax.random.key(seed)
    ks = jax.random.split(key, 24)
    n_pages = PAST_LEN // PAGE_SIZE
    n_phys = (DEC_BATCH * n_pages * 5) // 4
    f32 = jnp.float32
    nrm = lambda k, s, sc: jax.random.normal(k, s, f32) * sc
    page_table = jax.random.permutation(ks[5], n_phys)[:DEC_BATCH * n_pages]
    page_table = page_table.reshape(DEC_BATCH, n_pages).astype(jnp.int32)
    return {
        'x_prompt': nrm(ks[0], (BATCH, SEQ, D_MODEL), 1.0),
        'x_sample': nrm(ks[1], (DEC_BATCH, DEC_SEQ, D_MODEL), 1.0),
        'cache_k': nrm(ks[2], (DEPTH, n_phys, PAGE_SIZE, N_HEADS, 2 * HEAD_DIM), 1.0),
        'cache_v': nrm(ks[3], (DEPTH, n_phys, PAGE_SIZE, N_HEADS, V_DIM), 1.0),
        'state_pool': nrm(ks[4], (DEPTH, DEC_BATCH, POOL_BUF, POOL_WIDTH), 1.0),
        'page_table': page_table,
        'c_prompt': nrm(ks[6], (BATCH, D_MODEL), 1.0),
        'c_sample': nrm(ks[7], (DEC_BATCH, D_MODEL), 1.0),
        'rel_bias': nrm(ks[8], (NUM_BUCKETS, N_HEADS), 0.5),
        'w_ada': nrm(ks[9], (DEPTH, D_MODEL, 3 * D_MODEL), D_MODEL ** -0.5),
        'b_ada': nrm(ks[10], (DEPTH, 3 * D_MODEL), 0.02),
        'g_pre': 1.0 + nrm(ks[11], (DEPTH, D_MODEL), 0.05),
        'g_post': 1.0 + nrm(ks[12], (DEPTH, D_MODEL), 0.05),
        'w_in': nrm(ks[13], (DEPTH, D_MODEL, IN_WIDTH), D_MODEL ** -0.5),
        'w_pool_grp': nrm(ks[14], (DEPTH, POOL_GROUPS, POOL_GROUP_DIM, POOL_GROUP_DIM), POOL_GROUP_DIM ** -0.5),
        'pool_scale': 1.0 + nrm(ks[15], (DEPTH, POOL_WIDTH), 0.1),
        'w_proj_pool': nrm(ks[16], (DEPTH, POOL_WIDTH, D_MODEL), POOL_WIDTH ** -0.5),
        'lambda_q1': nrm(ks[17], (DEPTH, HEAD_DIM), 0.1),
        'lambda_k1': nrm(ks[18], (DEPTH, HEAD_DIM), 0.1),
        'lambda_q2': nrm(ks[19], (DEPTH, HEAD_DIM), 0.1),
        'lambda_k2': nrm(ks[20], (DEPTH, HEAD_DIM), 0.1),
        'g_subln': 1.0 + nrm(ks[21], (DEPTH, V_DIM), 0.05),
        'w_proj_attn': nrm(ks[22], (DEPTH, ATTN_WIDTH, D_MODEL), ATTN_WIDTH ** -0.5),
        'w_out': nrm(ks[23], (DEPTH, D_MODEL, D_MODEL), D_MODEL ** -0.5),
    }


def reference(x_prompt, x_sample, cache_k, cache_v, state_pool, page_table, c_prompt, c_sample,
              rel_bias, w_ada, b_ada, g_pre, g_post, w_in, w_pool_grp, pool_scale, w_proj_pool,
              lambda_q1, lambda_k1, lambda_q2, lambda_k2, g_subln, w_proj_attn, w_out):
    xp, xs = x_prompt, x_sample
    nkp, nvp, npp, nks, nvs, nps = [], [], [], [], [], []
    for l in range(DEPTH):
        lam_init = 0.8 - 0.6 * math.exp(-0.3 * l)
        lam = diff_lambda(lambda_q1[l], lambda_k1[l], lambda_q2[l], lambda_k2[l], lam_init)

        gate, u, pg, q, k, v, ag, mgp, mga = layer_inputs(xp, c_prompt, w_ada[l], b_ada[l], g_pre[l], w_in[l])
        u_ext = jnp.concatenate([jnp.zeros((xp.shape[0], POOL_BUF, POOL_WIDTH), u.dtype), u], axis=1)
        y_pool = pool_branch(u_ext, pg, 0, w_pool_grp[l], pool_scale[l], w_proj_pool[l])
        o = prompt_attention(q, k, v, rel_bias, lam)
        y_attn = attn_finish(o, ag, g_subln[l], lam_init, w_proj_attn[l])
        xp = layer_output(xp, y_pool, y_attn, mgp, mga, w_out[l], g_post[l], gate)
        nkp.append(k)
        nvp.append(v)
        npp.append(u_ext[:, -POOL_BUF:])

        gate, u, pg, q, k, v, ag, mgp, mga = layer_inputs(xs, c_sample, w_ada[l], b_ada[l], g_pre[l], w_in[l])
        u_ext = jnp.concatenate([state_pool[l].astype(u.dtype), u], axis=1)
        y_pool = pool_branch(u_ext, pg, PAST_LEN, w_pool_grp[l], pool_scale[l], w_proj_pool[l])
        o = sample_attention(q, k, v, cache_k, cache_v, l, page_table, rel_bias, lam)
        y_attn = attn_finish(o, ag, g_subln[l], lam_init, w_proj_attn[l])
        xs = layer_output(xs, y_pool, y_attn, mgp, mga, w_out[l], g_post[l], gate)
        nks.append(k)
        nvs.append(v)
        nps.append(u_ext[:, -POOL_BUF:])

    new_k_prompt = jnp.stack(nkp)
    new_v_prompt = jnp.stack(nvp)
    new_pool_prompt = jnp.stack(npp)
    new_k_sample = jnp.stack(nks)
    new_v_sample = jnp.stack(nvs)
    new_pool_sample = jnp.stack(nps)
    return (xp, xs, new_k_prompt, new_v_prompt, new_pool_prompt, new_k_sample, new_v_sample, new_pool_sample)
```

```python
import functools
import math

import jax
import jax.numpy as jnp
from jax import lax
from jax.experimental import pallas as pl
from jax.experimental.pallas import tpu as pltpu

N_HEADS = 8
HEAD_DIM = 64
V_DIM = 2 * HEAD_DIM
POOL_WINDOWS = (2, 4, 8, 16)
POOL_BUF = max(POOL_WINDOWS) - 1
NUM_BUCKETS = 32
MAX_EXACT = NUM_BUCKETS // 2
MAX_DISTANCE = 128
RMS_EPS = 1e-6
NEG_INF = -1e30
ATTN_SCALE = HEAD_DIM ** -0.5
LOG2E = math.log2(math.e)

BF16 = jnp.bfloat16
F32 = jnp.float32

V7X_VMEM_BYTES = 64 * 1024 * 1024
VMEM_LIMIT = V7X_VMEM_BYTES - 8 * 1024 * 1024

IN_ROWS = 512
ATTN_TILE = 512
POST_ROWS = 512
HALO_ROWS = 16
PAGES_PER_STEP = 8


def _silu(x):
    return x * (1.0 / (1.0 + jnp.exp(-x)))


def _sigmoid(x):
    return 1.0 / (1.0 + jnp.exp(-x))


def _rms(x, g):
    return x * lax.rsqrt(jnp.mean(x * x, axis=-1, keepdims=True) + RMS_EPS) * g


def _mod_kernel(c_ref, w_ref, b_ref, o_ref):
    a = _silu(c_ref[...]).astype(BF16)
    o_ref[...] = jnp.dot(a, w_ref[...].astype(BF16), preferred_element_type=F32) + b_ref[...]


def _modulation(c, w_ada, b_ada):
    n, d = c.shape
    return pl.pallas_call(
        _mod_kernel,
        grid=(3,),
        in_specs=[
            pl.BlockSpec((n, d), lambda j: (0, 0)),
            pl.BlockSpec((d, d), lambda j: (0, j)),
            pl.BlockSpec((1, d), lambda j: (0, j)),
        ],
        out_specs=pl.BlockSpec((n, d), lambda j: (0, j)),
        out_shape=jax.ShapeDtypeStruct((n, 3 * d), F32),
        name="modulation",
    )(c, w_ada, b_ada.reshape(1, 3 * d))


_IN_GROUPS = 8


def _in_kernel(x_ref, scale_ref, shift_ref, g_ref, w_ref,
               u_ref, pg_ref, q_ref, k_ref, v_ref, ag_ref, mgp_ref, mga_ref):
    d = x_ref.shape[-1]
    x = x_ref[0]
    h = (_rms(x, g_ref[...]) * (1.0 + scale_ref[0]) + shift_ref[0]).astype(BF16)

    def z(c):
        return jnp.dot(h, w_ref[:, c * d:(c + 1) * d], preferred_element_type=F32)

    u_ref[0] = z(0)
    pg_ref[0] = _silu(z(1)).astype(BF16)
    q_ref[0] = (z(2) * (ATTN_SCALE * LOG2E)).astype(BF16)
    k_ref[0] = z(3)
    v_ref[0] = z(4)
    ag_ref[0] = _silu(z(5)).astype(BF16)
    mgp_ref[0] = _sigmoid(z(6)).astype(BF16)
    mga_ref[0] = _sigmoid(z(7)).astype(BF16)


def _in_projection(x, scale, shift, g_pre, w_in_bf16, rows):
    b, t, d = x.shape
    mod_rows = rows if scale.shape[1] == t else 1
    mod_map = (lambda i, j: (i, j, 0)) if scale.shape[1] == t else (lambda i, j: (i, 0, 0))
    row_spec = pl.BlockSpec((1, rows, d), lambda i, j: (i, j, 0))
    out_dtypes = (F32, BF16, BF16, F32, F32, BF16, BF16, BF16)
    return pl.pallas_call(
        _in_kernel,
        grid=(b, t // rows),
        in_specs=[
            row_spec,
            pl.BlockSpec((1, mod_rows, d), mod_map),
            pl.BlockSpec((1, mod_rows, d), mod_map),
            pl.BlockSpec((1, d), lambda i, j: (0, 0)),
            pl.BlockSpec((d, _IN_GROUPS * d), lambda i, j: (0, 0), pipeline_mode=pl.Buffered(1)),
        ],
        out_specs=[row_spec] * _IN_GROUPS,
        out_shape=[jax.ShapeDtypeStruct((b, t, d), dt) for dt in out_dtypes],
        compiler_params=pltpu.CompilerParams(
            dimension_semantics=("parallel", "parallel"), vmem_limit_bytes=VMEM_LIMIT),
        name="in_projection",
    )(x, scale, shift, g_pre.reshape(1, d), w_in_bf16)


def _bias_kernel(tab_ref, n_ref, o_ref):
    h = pl.program_id(0)
    n = n_ref[...]
    nc = jnp.maximum(n, 0)
    nf = jnp.maximum(nc, 1).astype(F32)
    large = MAX_EXACT + (jnp.log(nf / MAX_EXACT) / math.log(MAX_DISTANCE / MAX_EXACT)
                         * (NUM_BUCKETS - MAX_EXACT)).astype(jnp.int32)
    large = jnp.minimum(large, NUM_BUCKETS - 1)
    bucket = jnp.where(nc < MAX_EXACT, nc, large)
    far = tab_ref[h, NUM_BUCKETS - 1]
    acc = jnp.zeros(n.shape, F32)
    for b in range(NUM_BUCKETS):
        acc = jnp.where(bucket == b, (tab_ref[h, b] - far) * LOG2E, acc)
    o_ref[0] = jnp.where(n >= 0, acc, NEG_INF)


def _bias_tiles(rel_bias, n):
    r, c = n.shape
    return pl.pallas_call(
        _bias_kernel,
        grid=(N_HEADS,),
        in_specs=[
            pl.BlockSpec(memory_space=pltpu.SMEM),
            pl.BlockSpec((r, c), lambda h: (0, 0)),
        ],
        out_specs=pl.BlockSpec((1, r, c), lambda h: (h, 0, 0)),
        out_shape=jax.ShapeDtypeStruct((N_HEADS, r, c), F32),
        name="bias_tiles",
    )(rel_bias.T, n)


def _lambda(lam_ref, lam_init):
    e1 = jnp.exp(jnp.sum(lam_ref[0:1, :] * lam_ref[1:2, :], axis=-1, keepdims=True))
    e2 = jnp.exp(jnp.sum(lam_ref[2:3, :] * lam_ref[3:4, :], axis=-1, keepdims=True))
    return e1 - e2 + lam_init


def _prompt_attn_kernel(lam_ref, g_ref, q_ref, k_ref, v_ref, bias_ref, o_ref,
                        kb_ref, vt_ref, acc_ref, ml_ref, *, lam_init):
    qi = pl.program_id(2)
    tile = q_ref.shape[1]
    seq = k_ref.shape[1]

    @pl.when(qi == 0)
    def _():
        for c in range(seq // tile):
            rows = pl.ds(c * tile, tile)
            kb_ref[rows, :] = k_ref[0, rows, :].astype(BF16)
            vt_ref[:, rows] = v_ref[0, rows, :].T.astype(BF16)

    qt = q_ref[0].astype(F32).T
    first = lax.broadcasted_iota(jnp.int32, qt.shape, 0) < HEAD_DIM
    qts = (jnp.where(first, qt, 0.0).astype(BF16), jnp.where(first, 0.0, qt).astype(BF16))

    ml_ref[0:2, :] = jnp.full((2, tile), NEG_INF, F32)
    ml_ref[2:4, :] = jnp.zeros((2, tile), F32)
    acc_ref[...] = jnp.zeros(acc_ref.shape, F32)

    def step(j, bias):
        rows = pl.ds(pl.multiple_of(j * tile, tile), tile)
        kt = kb_ref[rows, :]
        vt = vt_ref[:, rows]
        for m in range(2):
            s = jnp.dot(kt, qts[m], preferred_element_type=F32)
            if bias is not None:
                s = s + bias
            m_old = ml_ref[m:m + 1, :]
            m_new = jnp.maximum(m_old, jnp.max(s, axis=0, keepdims=True))
            p = jnp.exp2(s - m_new)
            corr = jnp.exp2(m_old - m_new)
            ml_ref[m:m + 1, :] = m_new
            ml_ref[2 + m:3 + m, :] = ml_ref[2 + m:3 + m, :] * corr + jnp.sum(p, axis=0, keepdims=True)
            acc_ref[m] = acc_ref[m] * corr + jnp.dot(vt, p.astype(BF16), preferred_element_type=F32)

    def far_body(j, carry):
        step(j, None)
        return carry

    lax.fori_loop(0, jnp.maximum(qi - 1, 0), far_body, 0)

    @pl.when(qi >= 1)
    def _():
        step(qi - 1, bias_ref[0, 1])

    step(qi, bias_ref[0, 0])

    lam = _lambda(lam_ref, lam_init)
    o = acc_ref[0] / ml_ref[2:3, :] - lam * (acc_ref[1] / ml_ref[3:4, :])
    o_ref[0] = (_rms(o.T, g_ref[...]) * (1.0 - lam_init)).astype(o_ref.dtype)


def _prompt_attention(lam_rows, g_subln, q, k, v, bias, lam_init):
    b, t, d = q.shape
    tile = min(ATTN_TILE, t)
    assert t % tile == 0 and tile >= MAX_DISTANCE and bias.shape == (N_HEADS, 2, tile, tile)
    return pl.pallas_call(
        functools.partial(_prompt_attn_kernel, lam_init=lam_init),
        grid=(b, N_HEADS, t // tile),
        in_specs=[
            pl.BlockSpec((4, HEAD_DIM), lambda i, h, j: (0, 0)),
            pl.BlockSpec((1, V_DIM), lambda i, h, j: (0, 0)),
            pl.BlockSpec((1, tile, V_DIM), lambda i, h, j: (i, j, h)),
            pl.BlockSpec((1, t, V_DIM), lambda i, h, j: (i, 0, h)),
            pl.BlockSpec((1, t, V_DIM), lambda i, h, j: (i, 0, h)),
            pl.BlockSpec((1, 2, tile, tile), lambda i, h, j: (h, 0, 0, 0)),
        ],
        out_specs=pl.BlockSpec((1, tile, V_DIM), lambda i, h, j: (i, j, h)),
        out_shape=jax.ShapeDtypeStruct((b, t, d), BF16),
        scratch_shapes=[
            pltpu.VMEM((t, V_DIM), BF16),
            pltpu.VMEM((V_DIM, t), BF16),
            pltpu.VMEM((2, V_DIM, tile), F32),
            pltpu.VMEM((8, tile), F32),
        ],
        compiler_params=pltpu.CompilerParams(
            dimension_semantics=("parallel", "parallel", "arbitrary"), vmem_limit_bytes=VMEM_LIMIT),
        name="prompt_attention",
    )(lam_rows, g_subln.reshape(1, V_DIM), q, k, v, bias)


_SROWS = N_HEADS * 8


def _sample_attn_kernel(pt_ref, lam_ref, g_ref, q_ref, kn_ref, vn_ref, bias_ref, *rest,
                        lam_init, pages):
    k_refs = rest[:pages]
    v_refs = rest[pages:2 * pages]
    o_ref = rest[2 * pages]
    w_ref, acc_ref, m_ref, l_ref = rest[2 * pages + 1:]
    c = pl.program_id(1)
    last = pl.num_programs(1) - 1
    page = k_refs[0].shape[1]
    width = q_ref.shape[-1]

    @pl.when(c == 0)
    def _():
        q8 = q_ref[0]
        qrep = jnp.concatenate([q8] * N_HEADS, axis=0)
        row = lax.broadcasted_iota(jnp.int32, qrep.shape, 0)
        col = lax.broadcasted_iota(jnp.int32, qrep.shape, 1)
        w_ref[...] = jnp.where(col // HEAD_DIM == row // 4, qrep, 0).astype(BF16)
        m_ref[...] = jnp.full(m_ref.shape, NEG_INF, F32)
        l_ref[...] = jnp.zeros(l_ref.shape, F32)
        acc_ref[...] = jnp.zeros(acc_ref.shape, F32)

    def update(s, values):
        m_old = m_ref[...]
        m_new = jnp.maximum(m_old, jnp.max(s, axis=-1, keepdims=True))
        p = jnp.exp2(s - m_new)
        corr = jnp.exp2(m_old - m_new)
        m_ref[...] = m_new
        l_ref[...] = l_ref[...] * corr + jnp.sum(p, axis=-1, keepdims=True)
        pb = p.astype(BF16)
        pv, off = None, 0
        for val in values:
            n = val.shape[0]
            part = jnp.dot(pb[:, off:off + n], val, preferred_element_type=F32)
            pv = part if pv is None else pv + part
            off += n
        diag = jnp.concatenate(
            [pv[h * 8:(h + 1) * 8, h * V_DIM:(h + 1) * V_DIM] for h in range(N_HEADS)], axis=0)
        acc_ref[...] = acc_ref[...] * corr + diag

    w = w_ref[...]
    nt = (((1,), (1,)), ((), ()))
    s = jnp.concatenate(
        [lax.dot_general(w, k_refs[j][0].astype(BF16), nt, preferred_element_type=F32)
         for j in range(pages)], axis=-1)
    s = s + jnp.where(c == last, bias_ref[:, :pages * page], 0.0)
    update(s, [v_refs[j][0].astype(BF16) for j in range(pages)])

    @pl.when(c == last)
    def _():
        pad = jnp.zeros((page - 8, width), F32)
        kn = jnp.concatenate([kn_ref[0], pad], axis=0).astype(BF16)
        vn = jnp.concatenate([vn_ref[0], pad], axis=0).astype(BF16)
        s_new = lax.dot_general(w, kn, nt, preferred_element_type=F32)
        update(s_new + bias_ref[:, pages * page:], [vn])
        lam = _lambda(lam_ref, lam_init)
        d = acc_ref[...] / l_ref[...]
        o = d - lam * pltpu.roll(d, _SROWS - 4, 0)
        o = _rms(o, g_ref[...]) * (1.0 - lam_init)
        for h in range(N_HEADS):
            o_ref[0, :, h * V_DIM:(h + 1) * V_DIM] = o[h * 8:(h + 1) * 8]


def _sample_attention(page_table, lam_rows, g_subln, q8, kn8, vn8, bias, cache_k, cache_v, lam_init):
    bd, n_pages = page_table.shape
    _, page, width = cache_k.shape
    pages = min(PAGES_PER_STEP, n_pages)
    assert n_pages % pages == 0

    def page_spec(j):
        return pl.BlockSpec((1, page, width), lambda b, c, pt: (pt[b, c * pages + j], 0, 0))

    small = pl.BlockSpec((1, 8, width), lambda b, c, pt: (b, 0, 0))
    grid_spec = pltpu.PrefetchScalarGridSpec(
        num_scalar_prefetch=1,
        grid=(bd, n_pages // pages),
        in_specs=[
            pl.BlockSpec((4, HEAD_DIM), lambda b, c, pt: (0, 0)),
            pl.BlockSpec((1, V_DIM), lambda b, c, pt: (0, 0)),
            small, small, small,
            pl.BlockSpec(bias.shape, lambda b, c, pt: (0, 0)),
        ] + [page_spec(j) for j in range(pages)] + [page_spec(j) for j in range(pages)],
        out_specs=pl.BlockSpec((1, 8, width), lambda b, c, pt: (b, 0, 0)),
        scratch_shapes=[
            pltpu.VMEM((_SROWS, width), BF16),
            pltpu.VMEM((_SROWS, V_DIM), F32),
            pltpu.VMEM((_SROWS, 1), F32),
            pltpu.VMEM((_SROWS, 1), F32),
        ],
    )
    return pl.pallas_call(
        functools.partial(_sample_attn_kernel, lam_init=lam_init, pages=pages),
        grid_spec=grid_spec,
        out_shape=jax.ShapeDtypeStruct((bd, 8, width), F32),
        compiler_params=pltpu.CompilerParams(
            dimension_semantics=("parallel", "arbitrary"), vmem_limit_bytes=VMEM_LIMIT),
        name="sample_attention",
    )(page_table, lam_rows, g_subln.reshape(1, V_DIM), q8, kn8, vn8, bias,
      *([cache_k] * pages), *([cache_v] * pages))


def _finish(d_groups, pg, o_n, ag, mgp, mga, x, gate, wg_ref, ps_ref, wpp_ref, wpa_ref, wo_ref, gp_ref):
    y = jnp.concatenate(
        [jnp.dot(d, wg_ref[g], preferred_element_type=F32) for g, d in enumerate(d_groups)], axis=-1)
    y = (y * ps_ref[...] * pg.astype(F32)).astype(BF16)
    y_pool = jnp.dot(y, wpp_ref[...], preferred_element_type=F32)
    a = (o_n.astype(F32) * ag.astype(F32)).astype(BF16)
    y_attn = jnp.dot(a, wpa_ref[...], preferred_element_type=F32)
    merged = (mgp.astype(F32) * y_pool + mga.astype(F32) * y_attn).astype(BF16)
    r = jnp.dot(merged, wo_ref[...], preferred_element_type=F32)
    return x + gate * _rms(r, gp_ref[...])


def _prompt_post_kernel(u_ref, halo_ref, pg_ref, o_ref, ag_ref, mgp_ref, mga_ref, x_ref, gate_ref,
                        wg_ref, ps_ref, wpp_ref, wpa_ref, wo_ref, gp_ref, y_ref, ext_ref):
    i = pl.program_id(1)
    rows = u_ref.shape[1]
    gdim = wg_ref.shape[1]
    u = u_ref[0]
    ext_ref[0:HALO_ROWS, :] = jnp.where(i == 0, 0.0, halo_ref[0])
    ext_ref[HALO_ROWS:, :] = u
    pos = i * rows + lax.broadcasted_iota(jnp.int32, (rows, 1), 0)
    d_groups = []
    for g, w in enumerate(POOL_WINDOWS):
        cols = slice(g * gdim, (g + 1) * gdim)
        s = u[:, cols]
        for back in range(1, w):
            s = s + ext_ref[HALO_ROWS - back:HALO_ROWS - back + rows, cols]
        cnt = jnp.minimum(pos + 1, w).astype(F32)
        d_groups.append((s / cnt - u[:, cols]).astype(BF16))
    y_ref[0] = _finish(d_groups, pg_ref[0], o_ref[0], ag_ref[0], mgp_ref[0], mga_ref[0], x_ref[0],
                       gate_ref[0], wg_ref, ps_ref, wpp_ref, wpa_ref, wo_ref, gp_ref)


def _sample_post_kernel(hist_ref, u_ref, pg_ref, o_ref, ag_ref, mgp_ref, mga_ref, x_ref, gate_ref,
                        wg_ref, ps_ref, wpp_ref, wpa_ref, wo_ref, gp_ref, y_ref, *, start_pos):
    n_tok = u_ref.shape[0]
    gdim = wg_ref.shape[1]
    ext = [hist_ref[j] for j in range(POOL_BUF)] + [u_ref[t] for t in range(n_tok)]
    d_groups = []
    for g, w in enumerate(POOL_WINDOWS):
        cols = slice(g * gdim, (g + 1) * gdim)
        per_tok = []
        for t in range(n_tok):
            s = ext[POOL_BUF + t][:, cols]
            for back in range(1, w):
                s = s + ext[POOL_BUF + t - back][:, cols]
            cnt = float(min(start_pos + t + 1, w))
            per_tok.append((s / cnt - ext[POOL_BUF + t][:, cols]).astype(BF16))
        d_groups.append(jnp.concatenate(per_tok, axis=0))
    y_ref[...] = _finish(d_groups, pg_ref[...], o_ref[...], ag_ref[...], mgp_ref[...], mga_ref[...],
                         x_ref[...], gate_ref[...], wg_ref, ps_ref, wpp_ref, wpa_ref, wo_ref, gp_ref)


def _weight_specs(d, gdim, index):
    return [
        pl.BlockSpec((len(POOL_WINDOWS), gdim, gdim), lambda *a: (0, 0, 0)),
        pl.BlockSpec((1, d), lambda *a: (0, 0)),
        pl.BlockSpec((d, d), lambda *a: (0, 0)),
        pl.BlockSpec((d, d), lambda *a: (0, 0)),
        pl.BlockSpec((d, d), lambda *a: (0, 0)),
        pl.BlockSpec((1, d), lambda *a: (0, 0)),
    ]


def _prompt_post(u, pg, o_n, ag, mgp, mga, x, gate, weights):
    b, t, d = x.shape
    rows = min(POST_ROWS, t)
    gdim = d // len(POOL_WINDOWS)
    row_spec = pl.BlockSpec((1, rows, d), lambda i, j: (i, j, 0))
    halo_spec = pl.BlockSpec(
        (1, HALO_ROWS, d), lambda i, j: (i, jnp.maximum(j * (rows // HALO_ROWS) - 1, 0), 0))
    return pl.pallas_call(
        _prompt_post_kernel,
        grid=(b, t // rows),
        in_specs=[row_spec, halo_spec] + [row_spec] * 6
        + [pl.BlockSpec((1, 1, d), lambda i, j: (i, 0, 0))] + _weight_specs(d, gdim, None),
        out_specs=row_spec,
        out_shape=jax.ShapeDtypeStruct((b, t, d), F32),
        scratch_shapes=[pltpu.VMEM((HALO_ROWS + rows, d), F32)],
        compiler_params=pltpu.CompilerParams(
            dimension_semantics=("parallel", "parallel"), vmem_limit_bytes=VMEM_LIMIT),
        name="prompt_post",
    )(u, u, pg, o_n, ag, mgp, mga, x, gate, *weights)


def _sample_post(hist, u, pg, o_n, ag, mgp, mga, x, gate, weights, start_pos):
    n_tok, bd, d = u.shape
    n = n_tok * bd
    gdim = d // len(POOL_WINDOWS)
    flat = pl.BlockSpec((n, d), lambda i: (0, 0))
    return pl.pallas_call(
        functools.partial(_sample_post_kernel, start_pos=start_pos),
        grid=(1,),
        in_specs=[pl.BlockSpec(hist.shape, lambda i: (0, 0, 0)), pl.BlockSpec(u.shape, lambda i: (0, 0, 0))]
        + [flat] * 7 + _weight_specs(d, gdim, None),
        out_specs=flat,
        out_shape=jax.ShapeDtypeStruct((n, d), F32),
        compiler_params=pltpu.CompilerParams(vmem_limit_bytes=VMEM_LIMIT),
        name="sample_post",
    )(hist, u, pg, o_n, ag, mgp, mga, x, gate, *weights)


def _token_major(a):
    return jnp.swapaxes(a, 0, 1)


def kernel(x_prompt, x_sample, cache_k, cache_v, state_pool, page_table, c_prompt, c_sample, rel_bias, w_ada, b_ada, g_pre, g_post, w_in, w_pool_grp, pool_scale, w_proj_pool, lambda_q1, lambda_k1, lambda_q2, lambda_k2, g_subln, w_proj_attn, w_out):
    depth = w_in.shape[0]
    b, t, d = x_prompt.shape
    bd, ts, _ = x_sample.shape
    n_pages = page_table.shape[1]
    n_phys, page = cache_k.shape[1], cache_k.shape[2]
    past_len = n_pages * page
    assert ts == 4 and d == N_HEADS * V_DIM

    tile = min(ATTN_TILE, t)
    ar = jnp.arange(tile, dtype=jnp.int32)
    n_prompt = jnp.stack([ar[None, :] - ar[:, None], tile + ar[None, :] - ar[:, None]])
    bias_prompt = _bias_tiles(rel_bias, n_prompt.reshape(2 * tile, tile)).reshape(N_HEADS, 2, tile, tile)

    pages = min(PAGES_PER_STEP, n_pages)
    tq = jnp.arange(8, dtype=jnp.int32)[:, None] % ts
    key = jnp.arange(pages * page, dtype=jnp.int32)[None, :] + (n_pages - pages) * page
    tk = jnp.arange(page, dtype=jnp.int32)[None, :]
    n_new = jnp.where(tk < ts, tq - tk, -1)
    n_sample = jnp.concatenate([past_len + tq - key, n_new], axis=1)
    bias_sample = _bias_tiles(rel_bias, n_sample).reshape(_SROWS, (pages + 1) * page)

    xp = x_prompt
    xs = _token_major(x_sample).reshape(1, ts * bd, d)
    outs = [[] for _ in range(6)]
    for l in range(depth):
        lam_init = 0.8 - 0.6 * math.exp(-0.3 * l)
        lam_rows = jnp.stack([lambda_q1[l], lambda_k1[l], lambda_q2[l], lambda_k2[l]])
        w_in_l = w_in[l].astype(BF16)
        weights = (w_pool_grp[l].astype(BF16), pool_scale[l].reshape(1, d), w_proj_pool[l].astype(BF16),
                   w_proj_attn[l].astype(BF16), w_out[l].astype(BF16), g_post[l].reshape(1, d))

        mod = _modulation(jnp.concatenate([c_prompt, c_sample]), w_ada[l], b_ada[l])
        shift, scale, gate = mod[:, :d], mod[:, d:2 * d], mod[:, 2 * d:]

        u, pg, q, k, v, ag, mgp, mga = _in_projection(
            xp, scale[:b, None], shift[:b, None], g_pre[l], w_in_l, min(IN_ROWS, t))
        o_n = _prompt_attention(lam_rows, g_subln[l], q, k, v, bias_prompt, lam_init)
        xp = _prompt_post(u, pg, o_n, ag, mgp, mga, xp, gate[:b, None], weights)
        outs[0].append(k.reshape(b, t, N_HEADS, V_DIM))
        outs[1].append(v.reshape(b, t, N_HEADS, V_DIM))
        outs[2].append(u[:, t - POOL_BUF:])

        tile_rows = lambda a: jnp.tile(a, (ts, 1))[None]
        u, pg, q, k, v, ag, mgp, mga = _in_projection(
            xs, tile_rows(scale[b:]), tile_rows(shift[b:]), g_pre[l], w_in_l, ts * bd)
        seq_major = lambda a: _token_major(a.reshape(ts, bd, d))
        pad8 = lambda a: jnp.concatenate([a, jnp.zeros_like(a)], axis=1)
        q_s = seq_major(q)
        o8 = _sample_attention(
            page_table, lam_rows, g_subln[l], jnp.concatenate([q_s, q_s], axis=1), pad8(seq_major(k)),
            pad8(seq_major(v)), bias_sample, cache_k[l].reshape(n_phys, page, d),
            cache_v[l].reshape(n_phys, page, d), lam_init)
        o_n = _token_major(o8[:, :ts]).reshape(ts * bd, d)
        hist = _token_major(state_pool[l])
        u_t = u.reshape(ts, bd, d)
        flat = lambda a: a.reshape(ts * bd, d)
        xs = _sample_post(hist, u_t, flat(pg), o_n, flat(ag), flat(mgp), flat(mga), flat(xs),
                          jnp.tile(gate[b:], (ts, 1)), weights, past_len).reshape(1, ts * bd, d)
        outs[3].append(seq_major(k).reshape(bd, ts, N_HEADS, V_DIM))
        outs[4].append(seq_major(v).reshape(bd, ts, N_HEADS, V_DIM))
        outs[5].append(jnp.concatenate([state_pool[l], seq_major(u)], axis=1)[:, ts:])

    y_sample = _token_major(xs.reshape(ts, bd, d))
    return (xp, y_sample) + tuple(jnp.stack(o) for o in outs)
```

```python
import functools
import math

import jax
import jax.numpy as jnp
from jax import lax
from jax.experimental import pallas as pl
from jax.experimental.pallas import tpu as pltpu

N_HEADS = 8
HEAD_DIM = 64
V_DIM = 2 * HEAD_DIM
POOL_WINDOWS = (2, 4, 8, 16)
POOL_BUF = max(POOL_WINDOWS) - 1
NUM_BUCKETS = 32
MAX_EXACT = NUM_BUCKETS // 2
MAX_DISTANCE = 128
RMS_EPS = 1e-6
NEG_INF = -1e30
ATTN_SCALE = HEAD_DIM ** -0.5
LOG2E = math.log2(math.e)

BF16 = jnp.bfloat16
F32 = jnp.float32

V7X_VMEM_BYTES = 64 * 1024 * 1024
VMEM_LIMIT = V7X_VMEM_BYTES - 8 * 1024 * 1024

IN_ROWS = 512
ATTN_TILE = 512
POST_ROWS = 512
HALO_ROWS = 16
PAGES_PER_STEP = 8


def _silu(x):
    return x * (1.0 / (1.0 + jnp.exp(-x)))


def _sigmoid(x):
    return 1.0 / (1.0 + jnp.exp(-x))


def _rms(x, g):
    return x * lax.rsqrt(jnp.mean(x * x, axis=-1, keepdims=True) + RMS_EPS) * g


def _mod_kernel(c_ref, w_ref, b_ref, o_ref):
    a = _silu(c_ref[...]).astype(BF16)
    o_ref[...] = jnp.dot(a, w_ref[...].astype(BF16), preferred_element_type=F32) + b_ref[...]


def _modulation(c, w_ada, b_ada):
    n, d = c.shape
    return pl.pallas_call(
        _mod_kernel,
        grid=(3,),
        in_specs=[
            pl.BlockSpec((n, d), lambda j: (0, 0)),
            pl.BlockSpec((d, d), lambda j: (0, j)),
            pl.BlockSpec((1, d), lambda j: (0, j)),
        ],
        out_specs=pl.BlockSpec((n, d), lambda j: (0, j)),
        out_shape=jax.ShapeDtypeStruct((n, 3 * d), F32),
        name="modulation",
    )(c, w_ada, b_ada.reshape(1, 3 * d))


_IN_GROUPS = 8


def _in_kernel(x_ref, scale_ref, shift_ref, g_ref, w_ref,
               u_ref, pg_ref, q_ref, k_ref, v_ref, ag_ref, mgp_ref, mga_ref):
    d = x_ref.shape[-1]
    x = x_ref[0]
    h = (_rms(x, g_ref[...]) * (1.0 + scale_ref[0]) + shift_ref[0]).astype(BF16)

    def z(c):
        return jnp.dot(h, w_ref[:, c * d:(c + 1) * d], preferred_element_type=F32)

    u_ref[0] = z(0)
    pg_ref[0] = _silu(z(1)).astype(BF16)
    q_ref[0] = (z(2) * (ATTN_SCALE * LOG2E)).astype(BF16)
    k_ref[0] = z(3)
    v_ref[0] = z(4)
    ag_ref[0] = _silu(z(5)).astype(BF16)
    mgp_ref[0] = _sigmoid(z(6)).astype(BF16)
    mga_ref[0] = _sigmoid(z(7)).astype(BF16)


def _in_projection(x, scale, shift, g_pre, w_in_bf16, rows):
    b, t, d = x.shape
    mod_rows = rows if scale.shape[1] == t else 1
    mod_map = (lambda i, j: (i, j, 0)) if scale.shape[1] == t else (lambda i, j: (i, 0, 0))
    row_spec = pl.BlockSpec((1, rows, d), lambda i, j: (i, j, 0))
    out_dtypes = (F32, BF16, BF16, F32, F32, BF16, BF16, BF16)
    return pl.pallas_call(
        _in_kernel,
        grid=(b, t // rows),
        in_specs=[
            row_spec,
            pl.BlockSpec((1, mod_rows, d), mod_map),
            pl.BlockSpec((1, mod_rows, d), mod_map),
            pl.BlockSpec((1, d), lambda i, j: (0, 0)),
            pl.BlockSpec((d, _IN_GROUPS * d), lambda i, j: (0, 0), pipeline_mode=pl.Buffered(1)),
        ],
        out_specs=[row_spec] * _IN_GROUPS,
        out_shape=[jax.ShapeDtypeStruct((b, t, d), dt) for dt in out_dtypes],
        compiler_params=pltpu.CompilerParams(
            dimension_semantics=("parallel", "parallel"), vmem_limit_bytes=VMEM_LIMIT),
        name="in_projection",
    )(x, scale, shift, g_pre.reshape(1, d), w_in_bf16)


def _bias_kernel(tab_ref, n_ref, o_ref):
    h = pl.program_id(0)
    n = n_ref[...]
    nc = jnp.maximum(n, 0)
    nf = jnp.maximum(nc, 1).astype(F32)
    large = MAX_EXACT + (jnp.log(nf / MAX_EXACT) / math.log(MAX_DISTANCE / MAX_EXACT)
                         * (NUM_BUCKETS - MAX_EXACT)).astype(jnp.int32)
    large = jnp.minimum(large, NUM_BUCKETS - 1)
    bucket = jnp.where(nc < MAX_EXACT, nc, large)
    far = tab_ref[h, NUM_BUCKETS - 1]
    acc = jnp.zeros(n.shape, F32)
    for b in range(NUM_BUCKETS):
        acc = jnp.where(bucket == b, (tab_ref[h, b] - far) * LOG2E, acc)
    o_ref[0] = jnp.where(n >= 0, acc, NEG_INF)


def _bias_tiles(rel_bias, n):
    r, c = n.shape
    return pl.pallas_call(
        _bias_kernel,
        grid=(N_HEADS,),
        in_specs=[
            pl.BlockSpec(memory_space=pltpu.SMEM),
            pl.BlockSpec((r, c), lambda h: (0, 0)),
        ],
        out_specs=pl.BlockSpec((1, r, c), lambda h: (h, 0, 0)),
        out_shape=jax.ShapeDtypeStruct((N_HEADS, r, c), F32),
        name="bias_tiles",
    )(rel_bias.T, n)


def _lambda(lam_ref, lam_init):
    e1 = jnp.exp(jnp.sum(lam_ref[0:1, :] * lam_ref[1:2, :], axis=-1, keepdims=True))
    e2 = jnp.exp(jnp.sum(lam_ref[2:3, :] * lam_ref[3:4, :], axis=-1, keepdims=True))
    return e1 - e2 + lam_init


def _prompt_attn_kernel(lam_ref, g_ref, q_ref, k_ref, v_ref, bias_ref, o_ref,
                        kb_ref, vt_ref, s_ref, acc_ref, ml_ref, *, lam_init):
    qi = pl.program_id(2)
    tile = q_ref.shape[1]
    seq = k_ref.shape[1]

    @pl.when(qi == 0)
    def _():
        for c in range(seq // tile):
            rows = pl.ds(c * tile, tile)
            kb_ref[rows, :] = k_ref[0, rows, :].astype(BF16)
            vt_ref[:, rows] = v_ref[0, rows, :].T.astype(BF16)

    qt = q_ref[0].astype(F32).T
    first = lax.broadcasted_iota(jnp.int32, qt.shape, 0) < HEAD_DIM
    qts = (jnp.where(first, qt, 0.0).astype(BF16), jnp.where(first, 0.0, qt).astype(BF16))

    ml_ref[0:2, :] = jnp.full((2, tile), NEG_INF, F32)
    ml_ref[2:4, :] = jnp.zeros((2, tile), F32)
    acc_ref[...] = jnp.zeros(acc_ref.shape, F32)

    def key_rows(back):
        j = jnp.maximum(qi - back, 0)
        return pl.ds(pl.multiple_of(j * tile, tile), tile)

    def score(back, slot):
        kt = kb_ref[key_rows(back), :]
        for m in range(2):
            s_ref[slot, m] = jnp.dot(kt, qts[m], preferred_element_type=F32)

    def consume(back, slot, bias):
        vt = vt_ref[:, key_rows(back)]
        for m in range(2):
            s = s_ref[slot, m]
            if bias is not None:
                s = s + bias
            m_old = ml_ref[m:m + 1, :]
            m_new = jnp.maximum(m_old, jnp.max(s, axis=0, keepdims=True))
            p = jnp.exp2(s - m_new)
            corr = jnp.exp2(m_old - m_new)
            ml_ref[m:m + 1, :] = m_new
            ml_ref[2 + m:3 + m, :] = ml_ref[2 + m:3 + m, :] * corr + jnp.sum(p, axis=0, keepdims=True)
            acc_ref[m] = acc_ref[m] * corr + jnp.dot(vt, p.astype(BF16), preferred_element_type=F32)

    score(0, 0)
    score(1, 1)
    consume(0, 0, bias_ref[0, 0])

    @pl.when(qi >= 1)
    def _():
        score(2, 0)
        consume(1, 1, bias_ref[0, 1])

    n_far = jnp.maximum(qi - 1, 0)

    def far_pair(p, carry):
        back = 2 * p + 2
        score(back + 1, 1)
        consume(back, 0, None)
        score(back + 2, 0)
        consume(back + 1, 1, None)
        return carry

    lax.fori_loop(0, n_far // 2, far_pair, 0)

    @pl.when(n_far % 2 == 1)
    def _():
        consume(qi, 0, None)

    lam = _lambda(lam_ref, lam_init)
    o = acc_ref[0] / ml_ref[2:3, :] - lam * (acc_ref[1] / ml_ref[3:4, :])
    o_ref[0] = (_rms(o.T, g_ref[...]) * (1.0 - lam_init)).astype(o_ref.dtype)


def _prompt_attention(lam_rows, g_subln, q, k, v, bias, lam_init):
    b, t, d = q.shape
    tile = min(ATTN_TILE, t)
    assert t % tile == 0 and tile >= MAX_DISTANCE and bias.shape == (N_HEADS, 2, tile, tile)
    return pl.pallas_call(
        functools.partial(_prompt_attn_kernel, lam_init=lam_init),
        grid=(b, N_HEADS, t // tile),
        in_specs=[
            pl.BlockSpec((4, HEAD_DIM), lambda i, h, j: (0, 0)),
            pl.BlockSpec((1, V_DIM), lambda i, h, j: (0, 0)),
            pl.BlockSpec((1, tile, V_DIM), lambda i, h, j: (i, j, h)),
            pl.BlockSpec((1, t, V_DIM), lambda i, h, j: (i, 0, h)),
            pl.BlockSpec((1, t, V_DIM), lambda i, h, j: (i, 0, h)),
            pl.BlockSpec((1, 2, tile, tile), lambda i, h, j: (h, 0, 0, 0)),
        ],
        out_specs=pl.BlockSpec((1, tile, V_DIM), lambda i, h, j: (i, j, h)),
        out_shape=jax.ShapeDtypeStruct((b, t, d), BF16),
        scratch_shapes=[
            pltpu.VMEM((t, V_DIM), BF16),
            pltpu.VMEM((V_DIM, t), BF16),
            pltpu.VMEM((2, 2, tile, tile), F32),
            pltpu.VMEM((2, V_DIM, tile), F32),
            pltpu.VMEM((8, tile), F32),
        ],
        compiler_params=pltpu.CompilerParams(
            dimension_semantics=("parallel", "parallel", "arbitrary"), vmem_limit_bytes=VMEM_LIMIT),
        name="prompt_attention",
    )(lam_rows, g_subln.reshape(1, V_DIM), q, k, v, bias)


_SROWS = N_HEADS * 8


def _sample_attn_kernel(pt_ref, lam_ref, g_ref, q_ref, kn_ref, vn_ref, bias_ref, *rest,
                        lam_init, pages):
    k_refs = rest[:pages]
    v_refs = rest[pages:2 * pages]
    o_ref = rest[2 * pages]
    w_ref, acc_ref, m_ref, l_ref = rest[2 * pages + 1:]
    c = pl.program_id(1)
    last = pl.num_programs(1) - 1
    page = k_refs[0].shape[1] // N_HEADS
    width = q_ref.shape[-1]

    def load_page(ref):
        return jnp.concatenate(
            [ref.at[0][pl.ds(h, page, stride=N_HEADS), :].astype(BF16) for h in range(N_HEADS)], axis=-1)

    @pl.when(c == 0)
    def _():
        q8 = q_ref[0]
        qrep = jnp.concatenate([q8] * N_HEADS, axis=0)
        row = lax.broadcasted_iota(jnp.int32, qrep.shape, 0)
        col = lax.broadcasted_iota(jnp.int32, qrep.shape, 1)
        w_ref[...] = jnp.where(col // HEAD_DIM == row // 4, qrep, 0).astype(BF16)
        m_ref[...] = jnp.full(m_ref.shape, NEG_INF, F32)
        l_ref[...] = jnp.zeros(l_ref.shape, F32)
        acc_ref[...] = jnp.zeros(acc_ref.shape, F32)

    def update(s, values):
        m_old = m_ref[...]
        m_new = jnp.maximum(m_old, jnp.max(s, axis=-1, keepdims=True))
        p = jnp.exp2(s - m_new)
        corr = jnp.exp2(m_old - m_new)
        m_ref[...] = m_new
        l_ref[...] = l_ref[...] * corr + jnp.sum(p, axis=-1, keepdims=True)
        pb = p.astype(BF16)
        pv, off = None, 0
        for val in values:
            n = val.shape[0]
            part = jnp.dot(pb[:, off:off + n], val, preferred_element_type=F32)
            pv = part if pv is None else pv + part
            off += n
        diag = jnp.concatenate(
            [pv[h * 8:(h + 1) * 8, h * V_DIM:(h + 1) * V_DIM] for h in range(N_HEADS)], axis=0)
        acc_ref[...] = acc_ref[...] * corr + diag

    w = w_ref[...]
    nt = (((1,), (1,)), ((), ()))
    s = jnp.concatenate(
        [lax.dot_general(w, load_page(k_refs[j]), nt, preferred_element_type=F32)
         for j in range(pages)], axis=-1)
    s = s + jnp.where(c == last, bias_ref[:, :pages * page], 0.0)
    update(s, [load_page(v_refs[j]) for j in range(pages)])

    @pl.when(c == last)
    def _():
        pad = jnp.zeros((page - 8, width), F32)
        kn = jnp.concatenate([kn_ref[0], pad], axis=0).astype(BF16)
        vn = jnp.concatenate([vn_ref[0], pad], axis=0).astype(BF16)
        s_new = lax.dot_general(w, kn, nt, preferred_element_type=F32)
        update(s_new + bias_ref[:, pages * page:], [vn])
        lam = _lambda(lam_ref, lam_init)
        d = acc_ref[...] / l_ref[...]
        o = d - lam * pltpu.roll(d, _SROWS - 4, 0)
        o = _rms(o, g_ref[...]) * (1.0 - lam_init)
        for h in range(N_HEADS):
            o_ref[0, :, h * V_DIM:(h + 1) * V_DIM] = o[h * 8:(h + 1) * 8]


def _sample_attention(page_table, lam_rows, g_subln, q8, kn8, vn8, bias, cache_k, cache_v, lam_init):
    bd, n_pages = page_table.shape
    page_rows, vdim = cache_k.shape[1:]
    width = q8.shape[-1]
    pages = min(PAGES_PER_STEP, n_pages)
    assert n_pages % pages == 0

    def page_spec(j):
        return pl.BlockSpec((1, page_rows, vdim), lambda b, c, pt: (pt[b, c * pages + j], 0, 0))

    small = pl.BlockSpec((1, 8, width), lambda b, c, pt: (b, 0, 0))
    grid_spec = pltpu.PrefetchScalarGridSpec(
        num_scalar_prefetch=1,
        grid=(bd, n_pages // pages),
        in_specs=[
            pl.BlockSpec((4, HEAD_DIM), lambda b, c, pt: (0, 0)),
            pl.BlockSpec((1, V_DIM), lambda b, c, pt: (0, 0)),
            small, small, small,
            pl.BlockSpec(bias.shape, lambda b, c, pt: (0, 0)),
        ] + [page_spec(j) for j in range(pages)] + [page_spec(j) for j in range(pages)],
        out_specs=pl.BlockSpec((1, 8, width), lambda b, c, pt: (b, 0, 0)),
        scratch_shapes=[
            pltpu.VMEM((_SROWS, width), BF16),
            pltpu.VMEM((_SROWS, V_DIM), F32),
            pltpu.VMEM((_SROWS, 1), F32),
            pltpu.VMEM((_SROWS, 1), F32),
        ],
    )
    return pl.pallas_call(
        functools.partial(_sample_attn_kernel, lam_init=lam_init, pages=pages),
        grid_spec=grid_spec,
        out_shape=jax.ShapeDtypeStruct((bd, 8, width), F32),
        compiler_params=pltpu.CompilerParams(
            dimension_semantics=("parallel", "arbitrary"), vmem_limit_bytes=VMEM_LIMIT),
        name="sample_attention",
    )(page_table, lam_rows, g_subln.reshape(1, V_DIM), q8, kn8, vn8, bias,
      *([cache_k] * pages), *([cache_v] * pages))


def _finish(d_groups, pg, o_n, ag, mgp, mga, x, gate, wg_ref, ps_ref, wpp_ref, wpa_ref, wo_ref, gp_ref):
    y = jnp.concatenate(
        [jnp.dot(d, wg_ref[g], preferred_element_type=F32) for g, d in enumerate(d_groups)], axis=-1)
    y = (y * ps_ref[...] * pg.astype(F32)).astype(BF16)
    y_pool = jnp.dot(y, wpp_ref[...], preferred_element_type=F32)
    a = (o_n.astype(F32) * ag.astype(F32)).astype(BF16)
    y_attn = jnp.dot(a, wpa_ref[...], preferred_element_type=F32)
    merged = (mgp.astype(F32) * y_pool + mga.astype(F32) * y_attn).astype(BF16)
    r = jnp.dot(merged, wo_ref[...], preferred_element_type=F32)
    return x + gate * _rms(r, gp_ref[...])


def _prompt_post_kernel(u_ref, halo_ref, pg_ref, o_ref, ag_ref, mgp_ref, mga_ref, x_ref, gate_ref,
                        wg_ref, ps_ref, wpp_ref, wpa_ref, wo_ref, gp_ref, y_ref, ext_ref):
    i = pl.program_id(1)
    rows = u_ref.shape[1]
    gdim = wg_ref.shape[1]
    u = u_ref[0]
    ext_ref[0:HALO_ROWS, :] = jnp.where(i == 0, 0.0, halo_ref[0])
    ext_ref[HALO_ROWS:, :] = u
    pos = i * rows + lax.broadcasted_iota(jnp.int32, (rows, 1), 0)
    d_groups = []
    for g, w in enumerate(POOL_WINDOWS):
        cols = slice(g * gdim, (g + 1) * gdim)
        s = u[:, cols]
        for back in range(1, w):
            s = s + ext_ref[HALO_ROWS - back:HALO_ROWS - back + rows, cols]
        cnt = jnp.minimum(pos + 1, w).astype(F32)
        d_groups.append((s / cnt - u[:, cols]).astype(BF16))
    y_ref[0] = _finish(d_groups, pg_ref[0], o_ref[0], ag_ref[0], mgp_ref[0], mga_ref[0], x_ref[0],
                       gate_ref[0], wg_ref, ps_ref, wpp_ref, wpa_ref, wo_ref, gp_ref)


def _sample_post_kernel(hist_ref, u_ref, pg_ref, o_ref, ag_ref, mgp_ref, mga_ref, x_ref, gate_ref,
                        wg_ref, ps_ref, wpp_ref, wpa_ref, wo_ref, gp_ref, y_ref, *, start_pos):
    n_tok = u_ref.shape[0]
    gdim = wg_ref.shape[1]
    ext = [hist_ref[j] for j in range(POOL_BUF)] + [u_ref[t] for t in range(n_tok)]
    d_groups = []
    for g, w in enumerate(POOL_WINDOWS):
        cols = slice(g * gdim, (g + 1) * gdim)
        per_tok = []
        for t in range(n_tok):
            s = ext[POOL_BUF + t][:, cols]
            for back in range(1, w):
                s = s + ext[POOL_BUF + t - back][:, cols]
            cnt = float(min(start_pos + t + 1, w))
            per_tok.append((s / cnt - ext[POOL_BUF + t][:, cols]).astype(BF16))
        d_groups.append(jnp.concatenate(per_tok, axis=0))
    y_ref[...] = _finish(d_groups, pg_ref[...], o_ref[...], ag_ref[...], mgp_ref[...], mga_ref[...],
                         x_ref[...], gate_ref[...], wg_ref, ps_ref, wpp_ref, wpa_ref, wo_ref, gp_ref)


def _weight_specs(d, gdim, index):
    return [
        pl.BlockSpec((len(POOL_WINDOWS), gdim, gdim), lambda *a: (0, 0, 0)),
        pl.BlockSpec((1, d), lambda *a: (0, 0)),
        pl.BlockSpec((d, d), lambda *a: (0, 0)),
        pl.BlockSpec((d, d), lambda *a: (0, 0)),
        pl.BlockSpec((d, d), lambda *a: (0, 0)),
        pl.BlockSpec((1, d), lambda *a: (0, 0)),
    ]


def _prompt_post(u, pg, o_n, ag, mgp, mga, x, gate, weights):
    b, t, d = x.shape
    rows = min(POST_ROWS, t)
    gdim = d // len(POOL_WINDOWS)
    row_spec = pl.BlockSpec((1, rows, d), lambda i, j: (i, j, 0))
    halo_spec = pl.BlockSpec(
        (1, HALO_ROWS, d), lambda i, j: (i, jnp.maximum(j * (rows // HALO_ROWS) - 1, 0), 0))
    return pl.pallas_call(
        _prompt_post_kernel,
        grid=(b, t // rows),
        in_specs=[row_spec, halo_spec] + [row_spec] * 6
        + [pl.BlockSpec((1, 1, d), lambda i, j: (i, 0, 0))] + _weight_specs(d, gdim, None),
        out_specs=row_spec,
        out_shape=jax.ShapeDtypeStruct((b, t, d), F32),
        scratch_shapes=[pltpu.VMEM((HALO_ROWS + rows, d), F32)],
        compiler_params=pltpu.CompilerParams(
            dimension_semantics=("parallel", "parallel"), vmem_limit_bytes=VMEM_LIMIT),
        name="prompt_post",
    )(u, u, pg, o_n, ag, mgp, mga, x, gate, *weights)


def _sample_post(hist, u, pg, o_n, ag, mgp, mga, x, gate, weights, start_pos):
    n_tok, bd, d = u.shape
    n = n_tok * bd
    gdim = d // len(POOL_WINDOWS)
    flat = pl.BlockSpec((n, d), lambda i: (0, 0))
    return pl.pallas_call(
        functools.partial(_sample_post_kernel, start_pos=start_pos),
        grid=(1,),
        in_specs=[pl.BlockSpec(hist.shape, lambda i: (0, 0, 0)), pl.BlockSpec(u.shape, lambda i: (0, 0, 0))]
        + [flat] * 7 + _weight_specs(d, gdim, None),
        out_specs=flat,
        out_shape=jax.ShapeDtypeStruct((n, d), F32),
        compiler_params=pltpu.CompilerParams(vmem_limit_bytes=VMEM_LIMIT),
        name="sample_post",
    )(hist, u, pg, o_n, ag, mgp, mga, x, gate, *weights)


def _token_major(a):
    return jnp.swapaxes(a, 0, 1)


def kernel(x_prompt, x_sample, cache_k, cache_v, state_pool, page_table, c_prompt, c_sample, rel_bias, w_ada, b_ada, g_pre, g_post, w_in, w_pool_grp, pool_scale, w_proj_pool, lambda_q1, lambda_k1, lambda_q2, lambda_k2, g_subln, w_proj_attn, w_out):
    depth = w_in.shape[0]
    b, t, d = x_prompt.shape
    bd, ts, _ = x_sample.shape
    n_pages = page_table.shape[1]
    n_phys, page = cache_k.shape[1], cache_k.shape[2]
    past_len = n_pages * page
    assert ts == 4 and d == N_HEADS * V_DIM

    tile = min(ATTN_TILE, t)
    ar = jnp.arange(tile, dtype=jnp.int32)
    n_prompt = jnp.stack([ar[None, :] - ar[:, None], tile + ar[None, :] - ar[:, None]])
    bias_prompt = _bias_tiles(rel_bias, n_prompt.reshape(2 * tile, tile)).reshape(N_HEADS, 2, tile, tile)

    pages = min(PAGES_PER_STEP, n_pages)
    tq = jnp.arange(8, dtype=jnp.int32)[:, None] % ts
    key = jnp.arange(pages * page, dtype=jnp.int32)[None, :] + (n_pages - pages) * page
    tk = jnp.arange(page, dtype=jnp.int32)[None, :]
    n_new = jnp.where(tk < ts, tq - tk, -1)
    n_sample = jnp.concatenate([past_len + tq - key, n_new], axis=1)
    bias_sample = _bias_tiles(rel_bias, n_sample).reshape(_SROWS, (pages + 1) * page)

    xp = x_prompt
    xs = _token_major(x_sample).reshape(1, ts * bd, d)
    outs = [[] for _ in range(6)]
    for l in range(depth):
        lam_init = 0.8 - 0.6 * math.exp(-0.3 * l)
        lam_rows = jnp.stack([lambda_q1[l], lambda_k1[l], lambda_q2[l], lambda_k2[l]])
        w_in_l = w_in[l].astype(BF16)
        weights = (w_pool_grp[l].astype(BF16), pool_scale[l].reshape(1, d), w_proj_pool[l].astype(BF16),
                   w_proj_attn[l].astype(BF16), w_out[l].astype(BF16), g_post[l].reshape(1, d))

        mod = _modulation(jnp.concatenate([c_prompt, c_sample]), w_ada[l], b_ada[l])
        shift, scale, gate = mod[:, :d], mod[:, d:2 * d], mod[:, 2 * d:]

        u, pg, q, k, v, ag, mgp, mga = _in_projection(
            xp, scale[:b, None], shift[:b, None], g_pre[l], w_in_l, min(IN_ROWS, t))
        o_n = _prompt_attention(lam_rows, g_subln[l], q, k, v, bias_prompt, lam_init)
        xp = _prompt_post(u, pg, o_n, ag, mgp, mga, xp, gate[:b, None], weights)
        outs[0].append(k.reshape(b, t, N_HEADS, V_DIM))
        outs[1].append(v.reshape(b, t, N_HEADS, V_DIM))
        outs[2].append(u[:, t - POOL_BUF:])

        tile_rows = lambda a: jnp.tile(a, (ts, 1))[None]
        u, pg, q, k, v, ag, mgp, mga = _in_projection(
            xs, tile_rows(scale[b:]), tile_rows(shift[b:]), g_pre[l], w_in_l, ts * bd)
        seq_major = lambda a: _token_major(a.reshape(ts, bd, d))
        pad8 = lambda a: jnp.concatenate([a, jnp.zeros_like(a)], axis=1)
        q_s = seq_major(q)
        o8 = _sample_attention(
            page_table, lam_rows, g_subln[l], jnp.concatenate([q_s, q_s], axis=1), pad8(seq_major(k)),
            pad8(seq_major(v)), bias_sample, cache_k[l].reshape(n_phys, page * N_HEADS, V_DIM),
            cache_v[l].reshape(n_phys, page * N_HEADS, V_DIM), lam_init)
        o_n = _token_major(o8[:, :ts]).reshape(ts * bd, d)
        hist = _token_major(state_pool[l])
        u_t = u.reshape(ts, bd, d)
        flat = lambda a: a.reshape(ts * bd, d)
        xs = _sample_post(hist, u_t, flat(pg), o_n, flat(ag), flat(mgp), flat(mga), flat(xs),
                          jnp.tile(gate[b:], (ts, 1)), weights, past_len).reshape(1, ts * bd, d)
        outs[3].append(seq_major(k).reshape(bd, ts, N_HEADS, V_DIM))
        outs[4].append(seq_major(v).reshape(bd, ts, N_HEADS, V_DIM))
        outs[5].append(jnp.concatenate([state_pool[l], seq_major(u)], axis=1)[:, ts:])

    y_sample = _token_major(xs.reshape(ts, bd, d))
    return (xp, y_sample) + tuple(jnp.stack(o) for o in outs)
```

```python
import functools
import math

import jax
import jax.numpy as jnp
from jax import lax
from jax.experimental import pallas as pl
from jax.experimental.pallas import tpu as pltpu

N_HEADS = 8
HEAD_DIM = 64
V_DIM = 2 * HEAD_DIM
POOL_WINDOWS = (2, 4, 8, 16)
POOL_BUF = max(POOL_WINDOWS) - 1
NUM_BUCKETS = 32
MAX_EXACT = NUM_BUCKETS // 2
MAX_DISTANCE = 128
RMS_EPS = 1e-6
NEG_INF = -1e30
ATTN_SCALE = HEAD_DIM ** -0.5
LOG2E = math.log2(math.e)

BF16 = jnp.bfloat16
F32 = jnp.float32

V7X_VMEM_BYTES = 64 * 1024 * 1024
VMEM_LIMIT = V7X_VMEM_BYTES - 8 * 1024 * 1024

IN_ROWS = 512
ATTN_TILE = 512
POST_ROWS = 512
HALO_ROWS = 16
PAGES_PER_STEP = 16


def _silu(x):
    return x * (1.0 / (1.0 + jnp.exp(-x)))


def _sigmoid(x):
    return 1.0 / (1.0 + jnp.exp(-x))


def _rms(x, g):
    return x * lax.rsqrt(jnp.mean(x * x, axis=-1, keepdims=True) + RMS_EPS) * g


def _mod_kernel(c_ref, w_ref, b_ref, o_ref):
    a = _silu(c_ref[...]).astype(BF16)
    o_ref[...] = jnp.dot(a, w_ref[...].astype(BF16), preferred_element_type=F32) + b_ref[...]


def _modulation(c, w_ada, b_ada):
    n, d = c.shape
    return pl.pallas_call(
        _mod_kernel,
        grid=(3,),
        in_specs=[
            pl.BlockSpec((n, d), lambda j: (0, 0)),
            pl.BlockSpec((d, d), lambda j: (0, j)),
            pl.BlockSpec((1, d), lambda j: (0, j)),
        ],
        out_specs=pl.BlockSpec((n, d), lambda j: (0, j)),
        out_shape=jax.ShapeDtypeStruct((n, 3 * d), F32),
        name="modulation",
    )(c, w_ada, b_ada.reshape(1, 3 * d))


_IN_GROUPS = 8


def _in_kernel(x_ref, scale_ref, shift_ref, g_ref, w_ref,
               u_ref, pg_ref, q_ref, k_ref, v_ref, ag_ref, mgp_ref, mga_ref, *attn_refs):
    d = x_ref.shape[-1]
    x = x_ref[0]
    h = (_rms(x, g_ref[...]) * (1.0 + scale_ref[0]) + shift_ref[0]).astype(BF16)

    def z(c):
        return jnp.dot(h, w_ref[:, c * d:(c + 1) * d], preferred_element_type=F32)

    u_ref[0] = z(0)
    pg_ref[0] = _silu(z(1)).astype(BF16)
    q_ref[0] = (z(2) * (ATTN_SCALE * LOG2E)).astype(BF16)
    zk, zv = z(3), z(4)
    k_ref[0] = zk
    v_ref[0] = zv
    if attn_refs:
        attn_refs[0][0] = zk.astype(BF16)
        attn_refs[1][0] = zv.T.astype(BF16)
    ag_ref[0] = _silu(z(5)).astype(BF16)
    mgp_ref[0] = _sigmoid(z(6)).astype(BF16)
    mga_ref[0] = _sigmoid(z(7)).astype(BF16)


def _in_projection(x, scale, shift, g_pre, w_in_bf16, rows, attn_layouts):
    b, t, d = x.shape
    mod_rows = rows if scale.shape[1] == t else 1
    mod_map = (lambda i, j: (i, j, 0)) if scale.shape[1] == t else (lambda i, j: (i, 0, 0))
    row_spec = pl.BlockSpec((1, rows, d), lambda i, j: (i, j, 0))
    out_dtypes = (F32, BF16, BF16, F32, F32, BF16, BF16, BF16)
    out_specs = [row_spec] * _IN_GROUPS
    out_shape = [jax.ShapeDtypeStruct((b, t, d), dt) for dt in out_dtypes]
    if attn_layouts:
        out_specs = out_specs + [row_spec, pl.BlockSpec((1, d, rows), lambda i, j: (i, 0, j))]
        out_shape = out_shape + [jax.ShapeDtypeStruct((b, t, d), BF16), jax.ShapeDtypeStruct((b, d, t), BF16)]
    return pl.pallas_call(
        _in_kernel,
        grid=(b, t // rows),
        in_specs=[
            row_spec,
            pl.BlockSpec((1, mod_rows, d), mod_map),
            pl.BlockSpec((1, mod_rows, d), mod_map),
            pl.BlockSpec((1, d), lambda i, j: (0, 0)),
            pl.BlockSpec((d, _IN_GROUPS * d), lambda i, j: (0, 0), pipeline_mode=pl.Buffered(1)),
        ],
        out_specs=out_specs,
        out_shape=out_shape,
        compiler_params=pltpu.CompilerParams(
            dimension_semantics=("parallel", "parallel"), vmem_limit_bytes=VMEM_LIMIT),
        name="in_projection",
    )(x, scale, shift, g_pre.reshape(1, d), w_in_bf16)


def _bias_kernel(tab_ref, n_ref, o_ref):
    h = pl.program_id(0)
    n = n_ref[...]
    nc = jnp.maximum(n, 0)
    nf = jnp.maximum(nc, 1).astype(F32)
    large = MAX_EXACT + (jnp.log(nf / MAX_EXACT) / math.log(MAX_DISTANCE / MAX_EXACT)
                         * (NUM_BUCKETS - MAX_EXACT)).astype(jnp.int32)
    large = jnp.minimum(large, NUM_BUCKETS - 1)
    bucket = jnp.where(nc < MAX_EXACT, nc, large)
    far = tab_ref[h, NUM_BUCKETS - 1]
    acc = jnp.zeros(n.shape, F32)
    for b in range(NUM_BUCKETS):
        acc = jnp.where(bucket == b, (tab_ref[h, b] - far) * LOG2E, acc)
    o_ref[0] = jnp.where(n >= 0, acc, NEG_INF)


def _bias_tiles(rel_bias, n):
    r, c = n.shape
    return pl.pallas_call(
        _bias_kernel,
        grid=(N_HEADS,),
        in_specs=[
            pl.BlockSpec(memory_space=pltpu.SMEM),
            pl.BlockSpec((r, c), lambda h: (0, 0)),
        ],
        out_specs=pl.BlockSpec((1, r, c), lambda h: (h, 0, 0)),
        out_shape=jax.ShapeDtypeStruct((N_HEADS, r, c), F32),
        name="bias_tiles",
    )(rel_bias.T, n)


def _lambda(lam_ref, lam_init):
    e1 = jnp.exp(jnp.sum(lam_ref[0:1, :] * lam_ref[1:2, :], axis=-1, keepdims=True))
    e2 = jnp.exp(jnp.sum(lam_ref[2:3, :] * lam_ref[3:4, :], axis=-1, keepdims=True))
    return e1 - e2 + lam_init


def _prompt_attn_kernel(lam_ref, g_ref, q_ref, kb_ref, vt_ref, bias_ref, o_ref,
                        s_ref, acc_ref, ml_ref, *, lam_init):
    qi = pl.program_id(2)
    tile = q_ref.shape[1]

    qt = q_ref[0].astype(F32).T
    first = lax.broadcasted_iota(jnp.int32, qt.shape, 0) < HEAD_DIM
    qts = (jnp.where(first, qt, 0.0).astype(BF16), jnp.where(first, 0.0, qt).astype(BF16))

    ml_ref[0:2, :] = jnp.full((2, tile), NEG_INF, F32)
    ml_ref[2:4, :] = jnp.zeros((2, tile), F32)
    acc_ref[...] = jnp.zeros(acc_ref.shape, F32)

    def key_rows(back):
        j = jnp.maximum(qi - back, 0)
        return pl.ds(pl.multiple_of(j * tile, tile), tile)

    def score(back, slot):
        kt = kb_ref[0, key_rows(back), :]
        for m in range(2):
            s_ref[slot, m] = jnp.dot(kt, qts[m], preferred_element_type=F32)

    def consume(back, slot, bias):
        vt = vt_ref[0, :, key_rows(back)]
        for m in range(2):
            s = s_ref[slot, m]
            if bias is not None:
                s = s + bias
            m_old = ml_ref[m:m + 1, :]
            m_new = jnp.maximum(m_old, jnp.max(s, axis=0, keepdims=True))
            p = jnp.exp2(s - m_new)
            corr = jnp.exp2(m_old - m_new)
            ml_ref[m:m + 1, :] = m_new
            ml_ref[2 + m:3 + m, :] = ml_ref[2 + m:3 + m, :] * corr + jnp.sum(p, axis=0, keepdims=True)
            acc_ref[m] = acc_ref[m] * corr + jnp.dot(vt, p.astype(BF16), preferred_element_type=F32)

    score(0, 0)
    score(1, 1)
    consume(0, 0, bias_ref[0, 0])

    @pl.when(qi >= 1)
    def _():
        score(2, 0)
        consume(1, 1, bias_ref[0, 1])

    n_far = jnp.maximum(qi - 1, 0)

    def far_pair(p, carry):
        back = 2 * p + 2
        score(back + 1, 1)
        consume(back, 0, None)
        score(back + 2, 0)
        consume(back + 1, 1, None)
        return carry

    lax.fori_loop(0, n_far // 2, far_pair, 0)

    @pl.when(n_far % 2 == 1)
    def _():
        consume(qi, 0, None)

    lam = _lambda(lam_ref, lam_init)
    o = acc_ref[0] / ml_ref[2:3, :] - lam * (acc_ref[1] / ml_ref[3:4, :])
    o_ref[0] = (_rms(o.T, g_ref[...]) * (1.0 - lam_init)).astype(o_ref.dtype)


def _prompt_attention(lam_rows, g_subln, q, kb, vt, bias, lam_init):
    b, t, d = q.shape
    tile = min(ATTN_TILE, t)
    assert t % tile == 0 and tile >= MAX_DISTANCE and bias.shape == (N_HEADS, 2, tile, tile)
    return pl.pallas_call(
        functools.partial(_prompt_attn_kernel, lam_init=lam_init),
        grid=(N_HEADS, b, t // tile),
        in_specs=[
            pl.BlockSpec((4, HEAD_DIM), lambda h, i, j: (0, 0)),
            pl.BlockSpec((1, V_DIM), lambda h, i, j: (0, 0)),
            pl.BlockSpec((1, tile, V_DIM), lambda h, i, j: (i, j, h)),
            pl.BlockSpec((1, t, V_DIM), lambda h, i, j: (i, 0, h)),
            pl.BlockSpec((1, V_DIM, t), lambda h, i, j: (i, h, 0)),
            pl.BlockSpec((1, 2, tile, tile), lambda h, i, j: (h, 0, 0, 0)),
        ],
        out_specs=pl.BlockSpec((1, tile, V_DIM), lambda h, i, j: (i, j, h)),
        out_shape=jax.ShapeDtypeStruct((b, t, d), BF16),
        scratch_shapes=[
            pltpu.VMEM((2, 2, tile, tile), F32),
            pltpu.VMEM((2, V_DIM, tile), F32),
            pltpu.VMEM((8, tile), F32),
        ],
        compiler_params=pltpu.CompilerParams(
            dimension_semantics=("parallel", "parallel", "arbitrary"), vmem_limit_bytes=VMEM_LIMIT),
        name="prompt_attention",
    )(lam_rows, g_subln.reshape(1, V_DIM), q, kb, vt, bias)


_SROWS = N_HEADS * 8


def _sample_attn_kernel(pt_ref, lam_ref, g_ref, q_ref, kn_ref, vn_ref, bias_ref, *rest,
                        lam_init, pages):
    k_refs = rest[:pages]
    v_refs = rest[pages:2 * pages]
    o_ref = rest[2 * pages]
    wq_ref, wn_ref, mask_ref, acc_ref, m_ref, l_ref = rest[2 * pages + 1:]
    c = pl.program_id(1)
    last = pl.num_programs(1) - 1
    page_rows = k_refs[0].shape[1]
    page = page_rows // N_HEADS
    width = q_ref.shape[-1]
    nt = (((1,), (1,)), ((), ()))

    @pl.when(c == 0)
    def _():
        q8 = q_ref[0]
        qh = jnp.concatenate([q8[:, h * V_DIM:(h + 1) * V_DIM] for h in range(N_HEADS)], axis=0)
        row = lax.broadcasted_iota(jnp.int32, qh.shape, 0)
        lane = lax.broadcasted_iota(jnp.int32, qh.shape, 1)
        wq_ref[...] = jnp.where(lane // HEAD_DIM == (row % 8) // 4, qh, 0).astype(BF16)
        qrep = jnp.concatenate([q8] * N_HEADS, axis=0)
        row = lax.broadcasted_iota(jnp.int32, qrep.shape, 0)
        col = lax.broadcasted_iota(jnp.int32, qrep.shape, 1)
        wn_ref[...] = jnp.where(col // HEAD_DIM == row // 4, qrep, 0).astype(BF16)
        row = lax.broadcasted_iota(jnp.int32, mask_ref.shape, 0)
        col = lax.broadcasted_iota(jnp.int32, mask_ref.shape, 1)
        mask_ref[...] = jnp.where(col % N_HEADS == row // 8, 0.0, NEG_INF)
        m_ref[...] = jnp.full(m_ref.shape, NEG_INF, F32)
        l_ref[...] = jnp.zeros(l_ref.shape, F32)
        acc_ref[...] = jnp.zeros(acc_ref.shape, F32)

    def update(s, values, own_head_blocks):
        m_old = m_ref[...]
        m_new = jnp.maximum(m_old, jnp.max(s, axis=-1, keepdims=True))
        p = jnp.exp2(s - m_new)
        corr = jnp.exp2(m_old - m_new)
        m_ref[...] = m_new
        l_ref[...] = l_ref[...] * corr + jnp.sum(p, axis=-1, keepdims=True)
        pb = p.astype(BF16)
        pv, off = None, 0
        for val in values:
            n = val.shape[0]
            part = jnp.dot(pb[:, off:off + n], val, preferred_element_type=F32)
            pv = part if pv is None else pv + part
            off += n
        if own_head_blocks:
            pv = jnp.concatenate(
                [pv[h * 8:(h + 1) * 8, h * V_DIM:(h + 1) * V_DIM] for h in range(N_HEADS)], axis=0)
        acc_ref[...] = acc_ref[...] * corr + pv

    wq = wq_ref[...]
    group = min(pages, 8)
    for g0 in range(0, pages, group):
        parts = []
        for j in range(g0, g0 + group):
            sj = lax.dot_general(wq, k_refs[j][0].astype(BF16), nt, preferred_element_type=F32)
            sj = sj + mask_ref[...]
            if j == pages - 1:
                sj = sj + jnp.where(c == last, bias_ref[:, :page_rows], 0.0)
            parts.append(sj)
        update(jnp.concatenate(parts, axis=-1),
               [v_refs[j][0].astype(BF16) for j in range(g0, g0 + group)], False)

    @pl.when(c == last)
    def _():
        pad = jnp.zeros((page - 8, width), F32)
        kn = jnp.concatenate([kn_ref[0], pad], axis=0).astype(BF16)
        vn = jnp.concatenate([vn_ref[0], pad], axis=0).astype(BF16)
        s_new = lax.dot_general(wn_ref[...], kn, nt, preferred_element_type=F32)
        update(s_new + bias_ref[:, page_rows:], [vn], True)
        lam = _lambda(lam_ref, lam_init)
        d = acc_ref[...] / l_ref[...]
        o = d - lam * pltpu.roll(d, _SROWS - 4, 0)
        o = _rms(o, g_ref[...]) * (1.0 - lam_init)
        for h in range(N_HEADS):
            o_ref[0, :, h * V_DIM:(h + 1) * V_DIM] = o[h * 8:(h + 1) * 8]


def _sample_attention(page_table, lam_rows, g_subln, q8, kn8, vn8, bias, cache_k, cache_v, lam_init):
    bd, n_pages = page_table.shape
    page_rows, vdim = cache_k.shape[1:]
    width = q8.shape[-1]
    pages = min(PAGES_PER_STEP, n_pages)
    assert n_pages % pages == 0 and (pages <= 8 or pages % 8 == 0)

    def page_spec(j):
        return pl.BlockSpec((1, page_rows, vdim), lambda b, c, pt: (pt[b, c * pages + j], 0, 0))

    small = pl.BlockSpec((1, 8, width), lambda b, c, pt: (b, 0, 0))
    grid_spec = pltpu.PrefetchScalarGridSpec(
        num_scalar_prefetch=1,
        grid=(bd, n_pages // pages),
        in_specs=[
            pl.BlockSpec((4, HEAD_DIM), lambda b, c, pt: (0, 0)),
            pl.BlockSpec((1, V_DIM), lambda b, c, pt: (0, 0)),
            small, small, small,
            pl.BlockSpec(bias.shape, lambda b, c, pt: (0, 0)),
        ] + [page_spec(j) for j in range(pages)] + [page_spec(j) for j in range(pages)],
        out_specs=pl.BlockSpec((1, 8, width), lambda b, c, pt: (b, 0, 0)),
        scratch_shapes=[
            pltpu.VMEM((_SROWS, V_DIM), BF16),
            pltpu.VMEM((_SROWS, width), BF16),
            pltpu.VMEM((_SROWS, page_rows), F32),
            pltpu.VMEM((_SROWS, V_DIM), F32),
            pltpu.VMEM((_SROWS, 1), F32),
            pltpu.VMEM((_SROWS, 1), F32),
        ],
    )
    return pl.pallas_call(
        functools.partial(_sample_attn_kernel, lam_init=lam_init, pages=pages),
        grid_spec=grid_spec,
        out_shape=jax.ShapeDtypeStruct((bd, 8, width), F32),
        compiler_params=pltpu.CompilerParams(
            dimension_semantics=("parallel", "arbitrary"), vmem_limit_bytes=VMEM_LIMIT),
        name="sample_attention",
    )(page_table, lam_rows, g_subln.reshape(1, V_DIM), q8, kn8, vn8, bias,
      *([cache_k] * pages), *([cache_v] * pages))


def _finish(d_groups, pg, o_n, ag, mgp, mga, x, gate, wg_ref, ps_ref, wpp_ref, wpa_ref, wo_ref, gp_ref):
    y = jnp.concatenate(
        [jnp.dot(d, wg_ref[g], preferred_element_type=F32) for g, d in enumerate(d_groups)], axis=-1)
    y = (y * ps_ref[...] * pg.astype(F32)).astype(BF16)
    y_pool = jnp.dot(y, wpp_ref[...], preferred_element_type=F32)
    a = (o_n.astype(F32) * ag.astype(F32)).astype(BF16)
    y_attn = jnp.dot(a, wpa_ref[...], preferred_element_type=F32)
    merged = (mgp.astype(F32) * y_pool + mga.astype(F32) * y_attn).astype(BF16)
    r = jnp.dot(merged, wo_ref[...], preferred_element_type=F32)
    return x + gate * _rms(r, gp_ref[...])


def _prompt_post_kernel(u_ref, halo_ref, pg_ref, o_ref, ag_ref, mgp_ref, mga_ref, x_ref, gate_ref,
                        wg_ref, ps_ref, wpp_ref, wpa_ref, wo_ref, gp_ref, y_ref, ext_ref):
    i = pl.program_id(1)
    rows = u_ref.shape[1]
    gdim = wg_ref.shape[1]
    u = u_ref[0]
    ext_ref[0:HALO_ROWS, :] = jnp.where(i == 0, 0.0, halo_ref[0])
    ext_ref[HALO_ROWS:, :] = u
    pos = i * rows + lax.broadcasted_iota(jnp.int32, (rows, 1), 0)
    d_groups = []
    for g, w in enumerate(POOL_WINDOWS):
        cols = slice(g * gdim, (g + 1) * gdim)
        s = u[:, cols]
        for back in range(1, w):
            s = s + ext_ref[HALO_ROWS - back:HALO_ROWS - back + rows, cols]
        cnt = jnp.minimum(pos + 1, w).astype(F32)
        d_groups.append((s / cnt - u[:, cols]).astype(BF16))
    y_ref[0] = _finish(d_groups, pg_ref[0], o_ref[0], ag_ref[0], mgp_ref[0], mga_ref[0], x_ref[0],
                       gate_ref[0], wg_ref, ps_ref, wpp_ref, wpa_ref, wo_ref, gp_ref)


def _sample_post_kernel(hist_ref, u_ref, pg_ref, o_ref, ag_ref, mgp_ref, mga_ref, x_ref, gate_ref,
                        wg_ref, ps_ref, wpp_ref, wpa_ref, wo_ref, gp_ref, y_ref, *, start_pos):
    n_tok = u_ref.shape[0]
    gdim = wg_ref.shape[1]
    ext = [hist_ref[j] for j in range(POOL_BUF)] + [u_ref[t] for t in range(n_tok)]
    d_groups = []
    for g, w in enumerate(POOL_WINDOWS):
        cols = slice(g * gdim, (g + 1) * gdim)
        per_tok = []
        for t in range(n_tok):
            s = ext[POOL_BUF + t][:, cols]
            for back in range(1, w):
                s = s + ext[POOL_BUF + t - back][:, cols]
            cnt = float(min(start_pos + t + 1, w))
            per_tok.append((s / cnt - ext[POOL_BUF + t][:, cols]).astype(BF16))
        d_groups.append(jnp.concatenate(per_tok, axis=0))
    y_ref[...] = _finish(d_groups, pg_ref[...], o_ref[...], ag_ref[...], mgp_ref[...], mga_ref[...],
                         x_ref[...], gate_ref[...], wg_ref, ps_ref, wpp_ref, wpa_ref, wo_ref, gp_ref)


def _weight_specs(d, gdim, index):
    return [
        pl.BlockSpec((len(POOL_WINDOWS), gdim, gdim), lambda *a: (0, 0, 0)),
        pl.BlockSpec((1, d), lambda *a: (0, 0)),
        pl.BlockSpec((d, d), lambda *a: (0, 0)),
        pl.BlockSpec((d, d), lambda *a: (0, 0)),
        pl.BlockSpec((d, d), lambda *a: (0, 0)),
        pl.BlockSpec((1, d), lambda *a: (0, 0)),
    ]


def _prompt_post(u, pg, o_n, ag, mgp, mga, x, gate, weights):
    b, t, d = x.shape
    rows = min(POST_ROWS, t)
    gdim = d // len(POOL_WINDOWS)
    row_spec = pl.BlockSpec((1, rows, d), lambda i, j: (i, j, 0))
    halo_spec = pl.BlockSpec(
        (1, HALO_ROWS, d), lambda i, j: (i, jnp.maximum(j * (rows // HALO_ROWS) - 1, 0), 0))
    return pl.pallas_call(
        _prompt_post_kernel,
        grid=(b, t // rows),
        in_specs=[row_spec, halo_spec] + [row_spec] * 6
        + [pl.BlockSpec((1, 1, d), lambda i, j: (i, 0, 0))] + _weight_specs(d, gdim, None),
        out_specs=row_spec,
        out_shape=jax.ShapeDtypeStruct((b, t, d), F32),
        scratch_shapes=[pltpu.VMEM((HALO_ROWS + rows, d), F32)],
        compiler_params=pltpu.CompilerParams(
            dimension_semantics=("parallel", "parallel"), vmem_limit_bytes=VMEM_LIMIT),
        name="prompt_post",
    )(u, u, pg, o_n, ag, mgp, mga, x, gate, *weights)


def _sample_post(hist, u, pg, o_n, ag, mgp, mga, x, gate, weights, start_pos):
    n_tok, bd, d = u.shape
    n = n_tok * bd
    gdim = d // len(POOL_WINDOWS)
    flat = pl.BlockSpec((n, d), lambda i: (0, 0))
    return pl.pallas_call(
        functools.partial(_sample_post_kernel, start_pos=start_pos),
        grid=(1,),
        in_specs=[pl.BlockSpec(hist.shape, lambda i: (0, 0, 0)), pl.BlockSpec(u.shape, lambda i: (0, 0, 0))]
        + [flat] * 7 + _weight_specs(d, gdim, None),
        out_specs=flat,
        out_shape=jax.ShapeDtypeStruct((n, d), F32),
        compiler_params=pltpu.CompilerParams(vmem_limit_bytes=VMEM_LIMIT),
        name="sample_post",
    )(hist, u, pg, o_n, ag, mgp, mga, x, gate, *weights)


def _token_major(a):
    return jnp.swapaxes(a, 0, 1)


def kernel(x_prompt, x_sample, cache_k, cache_v, state_pool, page_table, c_prompt, c_sample, rel_bias, w_ada, b_ada, g_pre, g_post, w_in, w_pool_grp, pool_scale, w_proj_pool, lambda_q1, lambda_k1, lambda_q2, lambda_k2, g_subln, w_proj_attn, w_out):
    depth = w_in.shape[0]
    b, t, d = x_prompt.shape
    bd, ts, _ = x_sample.shape
    n_pages = page_table.shape[1]
    n_phys, page = cache_k.shape[1], cache_k.shape[2]
    past_len = n_pages * page
    assert ts == 4 and d == N_HEADS * V_DIM

    tile = min(ATTN_TILE, t)
    ar = jnp.arange(tile, dtype=jnp.int32)
    n_prompt = jnp.stack([ar[None, :] - ar[:, None], tile + ar[None, :] - ar[:, None]])
    bias_prompt = _bias_tiles(rel_bias, n_prompt.reshape(2 * tile, tile)).reshape(N_HEADS, 2, tile, tile)

    tq = jnp.arange(8, dtype=jnp.int32)[:, None] % ts
    tk = jnp.arange(page, dtype=jnp.int32)[None, :]
    n_new = jnp.where(tk < ts, tq - tk, -1)
    n_sample = jnp.concatenate([page + tq - tk, n_new], axis=1)
    bias_sample = _bias_tiles(rel_bias, n_sample).reshape(_SROWS, 2 * page)
    bias_sample = jnp.concatenate(
        [jnp.repeat(bias_sample[:, :page], N_HEADS, axis=1), bias_sample[:, page:]], axis=1)

    xp = x_prompt
    xs = _token_major(x_sample).reshape(1, ts * bd, d)
    outs = [[] for _ in range(6)]
    for l in range(depth):
        lam_init = 0.8 - 0.6 * math.exp(-0.3 * l)
        lam_rows = jnp.stack([lambda_q1[l], lambda_k1[l], lambda_q2[l], lambda_k2[l]])
        w_in_l = w_in[l].astype(BF16)
        weights = (w_pool_grp[l].astype(BF16), pool_scale[l].reshape(1, d), w_proj_pool[l].astype(BF16),
                   w_proj_attn[l].astype(BF16), w_out[l].astype(BF16), g_post[l].reshape(1, d))

        mod = _modulation(jnp.concatenate([c_prompt, c_sample]), w_ada[l], b_ada[l])
        shift, scale, gate = mod[:, :d], mod[:, d:2 * d], mod[:, 2 * d:]

        u, pg, q, k, v, ag, mgp, mga, kb, vt = _in_projection(
            xp, scale[:b, None], shift[:b, None], g_pre[l], w_in_l, min(IN_ROWS, t), True)
        o_n = _prompt_attention(lam_rows, g_subln[l], q, kb, vt, bias_prompt, lam_init)
        xp = _prompt_post(u, pg, o_n, ag, mgp, mga, xp, gate[:b, None], weights)
        outs[0].append(k.reshape(b, t, N_HEADS, V_DIM))
        outs[1].append(v.reshape(b, t, N_HEADS, V_DIM))
        outs[2].append(u[:, t - POOL_BUF:])

        tile_rows = lambda a: jnp.tile(a, (ts, 1))[None]
        u, pg, q, k, v, ag, mgp, mga = _in_projection(
            xs, tile_rows(scale[b:]), tile_rows(shift[b:]), g_pre[l], w_in_l, ts * bd, False)
        seq_major = lambda a: _token_major(a.reshape(ts, bd, d))
        pad8 = lambda a: jnp.concatenate([a, jnp.zeros_like(a)], axis=1)
        q_s = seq_major(q)
        o8 = _sample_attention(
            page_table, lam_rows, g_subln[l], jnp.concatenate([q_s, q_s], axis=1), pad8(seq_major(k)),
            pad8(seq_major(v)), bias_sample, cache_k[l].reshape(n_phys, page * N_HEADS, V_DIM),
            cache_v[l].reshape(n_phys, page * N_HEADS, V_DIM), lam_init)
        o_n = _token_major(o8[:, :ts]).reshape(ts * bd, d)
        hist = _token_major(state_pool[l])
        u_t = u.reshape(ts, bd, d)
        flat = lambda a: a.reshape(ts * bd, d)
        xs = _sample_post(hist, u_t, flat(pg), o_n, flat(ag), flat(mgp), flat(mga), flat(xs),
                          jnp.tile(gate[b:], (ts, 1)), weights, past_len).reshape(1, ts * bd, d)
        outs[3].append(seq_major(k).reshape(bd, ts, N_HEADS, V_DIM))
        outs[4].append(seq_major(v).reshape(bd, ts, N_HEADS, V_DIM))
        outs[5].append(jnp.concatenate([state_pool[l], seq_major(u)], axis=1)[:, ts:])

    y_sample = _token_major(xs.reshape(ts, bd, d))
    return (xp, y_sample) + tuple(jnp.stack(o) for o in outs)
```

```python
import functools
import math

import jax
import jax.numpy as jnp
from jax import lax
from jax.experimental import pallas as pl
from jax.experimental.pallas import tpu as pltpu

N_HEADS = 8
HEAD_DIM = 64
V_DIM = 2 * HEAD_DIM
POOL_WINDOWS = (2, 4, 8, 16)
POOL_BUF = max(POOL_WINDOWS) - 1
NUM_BUCKETS = 32
MAX_EXACT = NUM_BUCKETS // 2
MAX_DISTANCE = 128
RMS_EPS = 1e-6
NEG_INF = -1e30
ATTN_SCALE = HEAD_DIM ** -0.5
LOG2E = math.log2(math.e)

BF16 = jnp.bfloat16
F32 = jnp.float32

V7X_VMEM_BYTES = 64 * 1024 * 1024
VMEM_LIMIT = V7X_VMEM_BYTES - 8 * 1024 * 1024

IN_ROWS = 512
ATTN_TILE = 512
POST_ROWS = 512
HALO_ROWS = 16


def _silu(x):
    return x * (1.0 / (1.0 + jnp.exp(-x)))


def _sigmoid(x):
    return 1.0 / (1.0 + jnp.exp(-x))


def _rms(x, g):
    return x * lax.rsqrt(jnp.mean(x * x, axis=-1, keepdims=True) + RMS_EPS) * g


def _mod_kernel(c_ref, w_ref, b_ref, o_ref):
    a = _silu(c_ref[...]).astype(BF16)
    o_ref[...] = jnp.dot(a, w_ref[...].astype(BF16), preferred_element_type=F32) + b_ref[...]


def _modulation(c, w_ada, b_ada):
    n, d = c.shape
    return pl.pallas_call(
        _mod_kernel,
        grid=(3,),
        in_specs=[
            pl.BlockSpec((n, d), lambda j: (0, 0)),
            pl.BlockSpec((d, d), lambda j: (0, j)),
            pl.BlockSpec((1, d), lambda j: (0, j)),
        ],
        out_specs=pl.BlockSpec((n, d), lambda j: (0, j)),
        out_shape=jax.ShapeDtypeStruct((n, 3 * d), F32),
        name="modulation",
    )(c, w_ada, b_ada.reshape(1, 3 * d))


_IN_GROUPS = 8


def _in_kernel(x_ref, scale_ref, shift_ref, g_ref, w_ref,
               u_ref, pg_ref, q_ref, k_ref, v_ref, ag_ref, mgp_ref, mga_ref, *attn_refs):
    d = x_ref.shape[-1]
    x = x_ref[0]
    h = (_rms(x, g_ref[...]) * (1.0 + scale_ref[0]) + shift_ref[0]).astype(BF16)

    def z(c):
        return jnp.dot(h, w_ref[:, c * d:(c + 1) * d], preferred_element_type=F32)

    u_ref[0] = z(0)
    pg_ref[0] = _silu(z(1)).astype(BF16)
    q_ref[0] = (z(2) * (ATTN_SCALE * LOG2E)).astype(BF16)
    zk, zv = z(3), z(4)
    k_ref[0] = zk
    v_ref[0] = zv
    if attn_refs:
        attn_refs[0][0] = zk.astype(BF16)
        attn_refs[1][0] = zv.T.astype(BF16)
    ag_ref[0] = _silu(z(5)).astype(BF16)
    mgp_ref[0] = _sigmoid(z(6)).astype(BF16)
    mga_ref[0] = _sigmoid(z(7)).astype(BF16)


def _in_projection(x, scale, shift, g_pre, w_in_bf16, rows, attn_layouts):
    b, t, d = x.shape
    mod_rows = rows if scale.shape[1] == t else 1
    mod_map = (lambda i, j: (i, j, 0)) if scale.shape[1] == t else (lambda i, j: (i, 0, 0))
    row_spec = pl.BlockSpec((1, rows, d), lambda i, j: (i, j, 0))
    out_dtypes = (F32, BF16, BF16, F32, F32, BF16, BF16, BF16)
    out_specs = [row_spec] * _IN_GROUPS
    out_shape = [jax.ShapeDtypeStruct((b, t, d), dt) for dt in out_dtypes]
    if attn_layouts:
        out_specs = out_specs + [row_spec, pl.BlockSpec((1, d, rows), lambda i, j: (i, 0, j))]
        out_shape = out_shape + [jax.ShapeDtypeStruct((b, t, d), BF16), jax.ShapeDtypeStruct((b, d, t), BF16)]
    return pl.pallas_call(
        _in_kernel,
        grid=(b, t // rows),
        in_specs=[
            row_spec,
            pl.BlockSpec((1, mod_rows, d), mod_map),
            pl.BlockSpec((1, mod_rows, d), mod_map),
            pl.BlockSpec((1, d), lambda i, j: (0, 0)),
            pl.BlockSpec((d, _IN_GROUPS * d), lambda i, j: (0, 0), pipeline_mode=pl.Buffered(1)),
        ],
        out_specs=out_specs,
        out_shape=out_shape,
        compiler_params=pltpu.CompilerParams(
            dimension_semantics=("parallel", "parallel"), vmem_limit_bytes=VMEM_LIMIT),
        name="in_projection",
    )(x, scale, shift, g_pre.reshape(1, d), w_in_bf16)


def _bias_kernel(tab_ref, n_ref, o_ref):
    h = pl.program_id(0)
    n = n_ref[...]
    nc = jnp.maximum(n, 0)
    nf = jnp.maximum(nc, 1).astype(F32)
    large = MAX_EXACT + (jnp.log(nf / MAX_EXACT) / math.log(MAX_DISTANCE / MAX_EXACT)
                         * (NUM_BUCKETS - MAX_EXACT)).astype(jnp.int32)
    large = jnp.minimum(large, NUM_BUCKETS - 1)
    bucket = jnp.where(nc < MAX_EXACT, nc, large)
    far = tab_ref[h, NUM_BUCKETS - 1]
    acc = jnp.zeros(n.shape, F32)
    for b in range(NUM_BUCKETS):
        acc = jnp.where(bucket == b, (tab_ref[h, b] - far) * LOG2E, acc)
    o_ref[0] = jnp.where(n >= 0, acc, NEG_INF)


def _bias_tiles(rel_bias, n):
    r, c = n.shape
    return pl.pallas_call(
        _bias_kernel,
        grid=(N_HEADS,),
        in_specs=[
            pl.BlockSpec(memory_space=pltpu.SMEM),
            pl.BlockSpec((r, c), lambda h: (0, 0)),
        ],
        out_specs=pl.BlockSpec((1, r, c), lambda h: (h, 0, 0)),
        out_shape=jax.ShapeDtypeStruct((N_HEADS, r, c), F32),
        name="bias_tiles",
    )(rel_bias.T, n)


def _lambda(lam_ref, lam_init):
    e1 = jnp.exp(jnp.sum(lam_ref[0:1, :] * lam_ref[1:2, :], axis=-1, keepdims=True))
    e2 = jnp.exp(jnp.sum(lam_ref[2:3, :] * lam_ref[3:4, :], axis=-1, keepdims=True))
    return e1 - e2 + lam_init


def _prompt_attn_body(qi, lam_ref, g_ref, q_ref, kb_ref, vt_ref, bias_ref, o_ref,
                      s_ref, acc_ref, ml_ref, lam_init, side_work):
    tile = q_ref.shape[1]

    qt = q_ref[0].astype(F32).T
    first = lax.broadcasted_iota(jnp.int32, qt.shape, 0) < HEAD_DIM
    qts = (jnp.where(first, qt, 0.0).astype(BF16), jnp.where(first, 0.0, qt).astype(BF16))

    ml_ref[0:2, :] = jnp.full((2, tile), NEG_INF, F32)
    ml_ref[2:4, :] = jnp.zeros((2, tile), F32)
    acc_ref[...] = jnp.zeros(acc_ref.shape, F32)

    def key_rows(back):
        j = jnp.maximum(qi - back, 0)
        return pl.ds(pl.multiple_of(j * tile, tile), tile)

    def score(back, slot):
        kt = kb_ref[0, key_rows(back), :]
        for m in range(2):
            s_ref[slot, m] = jnp.dot(kt, qts[m], preferred_element_type=F32)

    def consume(back, slot, bias):
        vt = vt_ref[0, :, key_rows(back)]
        for m in range(2):
            s = s_ref[slot, m]
            if bias is not None:
                s = s + bias
            m_old = ml_ref[m:m + 1, :]
            m_new = jnp.maximum(m_old, jnp.max(s, axis=0, keepdims=True))
            p = jnp.exp2(s - m_new)
            corr = jnp.exp2(m_old - m_new)
            ml_ref[m:m + 1, :] = m_new
            ml_ref[2 + m:3 + m, :] = ml_ref[2 + m:3 + m, :] * corr + jnp.sum(p, axis=0, keepdims=True)
            acc_ref[m] = acc_ref[m] * corr + jnp.dot(vt, p.astype(BF16), preferred_element_type=F32)

    score(0, 0)
    score(1, 1)
    consume(0, 0, bias_ref[0, 0])
    side_work()

    @pl.when(qi >= 1)
    def _():
        score(2, 0)
        consume(1, 1, bias_ref[0, 1])

    n_far = jnp.maximum(qi - 1, 0)

    def far_pair(p, carry):
        back = 2 * p + 2
        score(back + 1, 1)
        consume(back, 0, None)
        score(back + 2, 0)
        consume(back + 1, 1, None)
        return carry

    lax.fori_loop(0, n_far // 2, far_pair, 0)

    @pl.when(n_far % 2 == 1)
    def _():
        consume(qi, 0, None)

    lam = _lambda(lam_ref, lam_init)
    o = acc_ref[0] / ml_ref[2:3, :] - lam * (acc_ref[1] / ml_ref[3:4, :])
    o_ref[0] = (_rms(o.T, g_ref[...]) * (1.0 - lam_init)).astype(o_ref.dtype)


_SROWS = N_HEADS * 8


def _sample_attn_steps(c, last, lam_ref, g_ref, q_ref, kn_ref, vn_ref, bias_ref, k_refs, v_refs, o_ref,
                       wq_ref, wn_ref, mask_ref, acc_ref, m_ref, l_ref, lam_init):
    pages = len(k_refs)
    page_rows = k_refs[0].shape[1]
    page = page_rows // N_HEADS
    width = q_ref.shape[-1]
    nt = (((1,), (1,)), ((), ()))

    def start_body():
        q8 = q_ref[0]
        qh = jnp.concatenate([q8[:, h * V_DIM:(h + 1) * V_DIM] for h in range(N_HEADS)], axis=0)
        row = lax.broadcasted_iota(jnp.int32, qh.shape, 0)
        lane = lax.broadcasted_iota(jnp.int32, qh.shape, 1)
        wq_ref[...] = jnp.where(lane // HEAD_DIM == (row % 8) // 4, qh, 0).astype(BF16)
        qrep = jnp.concatenate([q8] * N_HEADS, axis=0)
        row = lax.broadcasted_iota(jnp.int32, qrep.shape, 0)
        col = lax.broadcasted_iota(jnp.int32, qrep.shape, 1)
        wn_ref[...] = jnp.where(col // HEAD_DIM == row // 4, qrep, 0).astype(BF16)
        row = lax.broadcasted_iota(jnp.int32, mask_ref.shape, 0)
        col = lax.broadcasted_iota(jnp.int32, mask_ref.shape, 1)
        mask_ref[...] = jnp.where(col % N_HEADS == row // 8, 0.0, NEG_INF)
        m_ref[...] = jnp.full(m_ref.shape, NEG_INF, F32)
        l_ref[...] = jnp.zeros(l_ref.shape, F32)
        acc_ref[...] = jnp.zeros(acc_ref.shape, F32)

    def update(s, values, own_head_blocks):
        m_old = m_ref[...]
        m_new = jnp.maximum(m_old, jnp.max(s, axis=-1, keepdims=True))
        p = jnp.exp2(s - m_new)
        corr = jnp.exp2(m_old - m_new)
        m_ref[...] = m_new
        l_ref[...] = l_ref[...] * corr + jnp.sum(p, axis=-1, keepdims=True)
        pb = p.astype(BF16)
        pv, off = None, 0
        for val in values:
            n = val.shape[0]
            part = jnp.dot(pb[:, off:off + n], val, preferred_element_type=F32)
            pv = part if pv is None else pv + part
            off += n
        if own_head_blocks:
            pv = jnp.concatenate(
                [pv[h * 8:(h + 1) * 8, h * V_DIM:(h + 1) * V_DIM] for h in range(N_HEADS)], axis=0)
        acc_ref[...] = acc_ref[...] * corr + pv

    def pages_step():
        wq = wq_ref[...]
        group = min(pages, 8)
        for g0 in range(0, pages, group):
            parts = []
            for j in range(g0, g0 + group):
                sj = lax.dot_general(wq, k_refs[j][0].astype(BF16), nt, preferred_element_type=F32)
                sj = sj + mask_ref[...]
                if j == pages - 1:
                    sj = sj + jnp.where(c == last, bias_ref[:, :page_rows], 0.0)
                parts.append(sj)
            update(jnp.concatenate(parts, axis=-1),
                   [v_refs[j][0].astype(BF16) for j in range(g0, g0 + group)], False)

    def finish_body():
        pad = jnp.zeros((page - 8, width), F32)
        kn = jnp.concatenate([kn_ref[0], pad], axis=0).astype(BF16)
        vn = jnp.concatenate([vn_ref[0], pad], axis=0).astype(BF16)
        s_new = lax.dot_general(wn_ref[...], kn, nt, preferred_element_type=F32)
        update(s_new + bias_ref[:, page_rows:], [vn], True)
        lam = _lambda(lam_ref, lam_init)
        d = acc_ref[...] / l_ref[...]
        o = d - lam * pltpu.roll(d, _SROWS - 4, 0)
        o = _rms(o, g_ref[...]) * (1.0 - lam_init)
        for h in range(N_HEADS):
            o_ref[0, :, h * V_DIM:(h + 1) * V_DIM] = o[h * 8:(h + 1) * 8]

    return (lambda: pl.when(c == 0)(start_body)), pages_step, (lambda: pl.when(c == last)(finish_body))


def _attn_kernel(pt_ref, lam_ref, g_ref, q_ref, kb_ref, vt_ref, biasp_ref, qs_ref, kn_ref, vn_ref,
                 biass_ref, *rest, lam_init, pages, chunks):
    del pt_ref
    k_refs = rest[:pages]
    v_refs = rest[pages:2 * pages]
    o_ref, os_ref = rest[2 * pages:2 * pages + 2]
    s_ref, acc_ref, ml_ref, wq_ref, wn_ref, mask_ref, accs_ref, ms_ref, ls_ref = rest[2 * pages + 2:]
    step = (pl.program_id(0) * pl.num_programs(1) + pl.program_id(1)) * pl.num_programs(2) + pl.program_id(2)
    sample_start, sample_pages, sample_finish = _sample_attn_steps(
        step % chunks, chunks - 1, lam_ref, g_ref, qs_ref, kn_ref, vn_ref, biass_ref,
        k_refs, v_refs, os_ref, wq_ref, wn_ref, mask_ref, accs_ref, ms_ref, ls_ref, lam_init)
    sample_start()
    _prompt_attn_body(pl.program_id(2), lam_ref, g_ref, q_ref, kb_ref, vt_ref, biasp_ref, o_ref,
                      s_ref, acc_ref, ml_ref, lam_init, sample_pages)
    sample_finish()


def _attention(page_table, lam_rows, g_subln, q, kb, vt, bias_prompt, q8, kn8, vn8, bias_sample,
               cache_k, cache_v, lam_init):
    b, t, d = q.shape
    bd, n_pages = page_table.shape
    page_rows, vdim = cache_k.shape[1:]
    tile = min(ATTN_TILE, t)
    nq = t // tile
    assert t % tile == 0 and tile >= MAX_DISTANCE and bias_prompt.shape == (N_HEADS, 2, tile, tile)
    steps = N_HEADS * b * nq
    assert (bd * n_pages) % steps == 0
    pages = bd * n_pages // steps
    assert n_pages % pages == 0 and (pages <= 8 or pages % 8 == 0)
    chunks = n_pages // pages

    def seq_of(h, i, j):
        return ((h * b + i) * nq + j) // chunks

    def page_spec(p):
        def index(h, i, j, pt):
            step = (h * b + i) * nq + j
            return (pt[step // chunks, (step % chunks) * pages + p], 0, 0)
        return pl.BlockSpec((1, page_rows, vdim), index)

    small = pl.BlockSpec((1, 8, d), lambda h, i, j, pt: (seq_of(h, i, j), 0, 0))
    grid_spec = pltpu.PrefetchScalarGridSpec(
        num_scalar_prefetch=1,
        grid=(N_HEADS, b, nq),
        in_specs=[
            pl.BlockSpec((4, HEAD_DIM), lambda h, i, j, pt: (0, 0)),
            pl.BlockSpec((1, V_DIM), lambda h, i, j, pt: (0, 0)),
            pl.BlockSpec((1, tile, V_DIM), lambda h, i, j, pt: (i, j, h)),
            pl.BlockSpec((1, t, V_DIM), lambda h, i, j, pt: (i, 0, h)),
            pl.BlockSpec((1, V_DIM, t), lambda h, i, j, pt: (i, h, 0)),
            pl.BlockSpec((1, 2, tile, tile), lambda h, i, j, pt: (h, 0, 0, 0), pipeline_mode=pl.Buffered(1)),
            small, small, small,
            pl.BlockSpec(bias_sample.shape, lambda h, i, j, pt: (0, 0)),
        ] + [page_spec(p) for p in range(pages)] + [page_spec(p) for p in range(pages)],
        out_specs=[
            pl.BlockSpec((1, tile, V_DIM), lambda h, i, j, pt: (i, j, h)),
            pl.BlockSpec((1, 8, d), lambda h, i, j, pt: (seq_of(h, i, j), 0, 0)),
        ],
        scratch_shapes=[
            pltpu.VMEM((2, 2, tile, tile), F32),
            pltpu.VMEM((2, V_DIM, tile), F32),
            pltpu.VMEM((8, tile), F32),
            pltpu.VMEM((_SROWS, V_DIM), BF16),
            pltpu.VMEM((_SROWS, d), BF16),
            pltpu.VMEM((_SROWS, page_rows), F32),
            pltpu.VMEM((_SROWS, V_DIM), F32),
            pltpu.VMEM((_SROWS, 1), F32),
            pltpu.VMEM((_SROWS, 1), F32),
        ],
    )
    return pl.pallas_call(
        functools.partial(_attn_kernel, lam_init=lam_init, pages=pages, chunks=chunks),
        grid_spec=grid_spec,
        out_shape=[jax.ShapeDtypeStruct((b, t, d), BF16), jax.ShapeDtypeStruct((bd, 8, d), F32)],
        compiler_params=pltpu.CompilerParams(
            dimension_semantics=("arbitrary", "arbitrary", "arbitrary"), vmem_limit_bytes=VMEM_LIMIT),
        name="attention",
    )(page_table, lam_rows, g_subln.reshape(1, V_DIM), q, kb, vt, bias_prompt, q8, kn8, vn8, bias_sample,
      *([cache_k] * pages), *([cache_v] * pages))


def _finish(d_groups, pg, o_n, ag, mgp, mga, x, gate, wg_ref, ps_ref, wpp_ref, wpa_ref, wo_ref, gp_ref):
    y = jnp.concatenate(
        [jnp.dot(d, wg_ref[g], preferred_element_type=F32) for g, d in enumerate(d_groups)], axis=-1)
    y = (y * ps_ref[...] * pg.astype(F32)).astype(BF16)
    y_pool = jnp.dot(y, wpp_ref[...], preferred_element_type=F32)
    a = (o_n.astype(F32) * ag.astype(F32)).astype(BF16)
    y_attn = jnp.dot(a, wpa_ref[...], preferred_element_type=F32)
    merged = (mgp.astype(F32) * y_pool + mga.astype(F32) * y_attn).astype(BF16)
    r = jnp.dot(merged, wo_ref[...], preferred_element_type=F32)
    return x + gate * _rms(r, gp_ref[...])


def _prompt_post_kernel(u_ref, halo_ref, pg_ref, o_ref, ag_ref, mgp_ref, mga_ref, x_ref, gate_ref,
                        wg_ref, ps_ref, wpp_ref, wpa_ref, wo_ref, gp_ref, y_ref, ext_ref):
    i = pl.program_id(1)
    rows = u_ref.shape[1]
    gdim = wg_ref.shape[1]
    u = u_ref[0]
    ext_ref[0:HALO_ROWS, :] = jnp.where(i == 0, 0.0, halo_ref[0])
    ext_ref[HALO_ROWS:, :] = u
    pos = i * rows + lax.broadcasted_iota(jnp.int32, (rows, 1), 0)
    d_groups = []
    for g, w in enumerate(POOL_WINDOWS):
        cols = slice(g * gdim, (g + 1) * gdim)
        s = u[:, cols]
        for back in range(1, w):
            s = s + ext_ref[HALO_ROWS - back:HALO_ROWS - back + rows, cols]
        cnt = jnp.minimum(pos + 1, w).astype(F32)
        d_groups.append((s / cnt - u[:, cols]).astype(BF16))
    y_ref[0] = _finish(d_groups, pg_ref[0], o_ref[0], ag_ref[0], mgp_ref[0], mga_ref[0], x_ref[0],
                       gate_ref[0], wg_ref, ps_ref, wpp_ref, wpa_ref, wo_ref, gp_ref)


def _sample_post_kernel(hist_ref, u_ref, pg_ref, o_ref, ag_ref, mgp_ref, mga_ref, x_ref, gate_ref,
                        wg_ref, ps_ref, wpp_ref, wpa_ref, wo_ref, gp_ref, y_ref, *, start_pos):
    n_tok = u_ref.shape[0]
    gdim = wg_ref.shape[1]
    ext = [hist_ref[j] for j in range(POOL_BUF)] + [u_ref[t] for t in range(n_tok)]
    d_groups = []
    for g, w in enumerate(POOL_WINDOWS):
        cols = slice(g * gdim, (g + 1) * gdim)
        per_tok = []
        for t in range(n_tok):
            s = ext[POOL_BUF + t][:, cols]
            for back in range(1, w):
                s = s + ext[POOL_BUF + t - back][:, cols]
            cnt = float(min(start_pos + t + 1, w))
            per_tok.append((s / cnt - ext[POOL_BUF + t][:, cols]).astype(BF16))
        d_groups.append(jnp.concatenate(per_tok, axis=0))
    y_ref[...] = _finish(d_groups, pg_ref[...], o_ref[...], ag_ref[...], mgp_ref[...], mga_ref[...],
                         x_ref[...], gate_ref[...], wg_ref, ps_ref, wpp_ref, wpa_ref, wo_ref, gp_ref)


def _weight_specs(d, gdim, index):
    return [
        pl.BlockSpec((len(POOL_WINDOWS), gdim, gdim), lambda *a: (0, 0, 0)),
        pl.BlockSpec((1, d), lambda *a: (0, 0)),
        pl.BlockSpec((d, d), lambda *a: (0, 0)),
        pl.BlockSpec((d, d), lambda *a: (0, 0)),
        pl.BlockSpec((d, d), lambda *a: (0, 0)),
        pl.BlockSpec((1, d), lambda *a: (0, 0)),
    ]


def _prompt_post(u, pg, o_n, ag, mgp, mga, x, gate, weights):
    b, t, d = x.shape
    rows = min(POST_ROWS, t)
    gdim = d // len(POOL_WINDOWS)
    row_spec = pl.BlockSpec((1, rows, d), lambda i, j: (i, j, 0))
    halo_spec = pl.BlockSpec(
        (1, HALO_ROWS, d), lambda i, j: (i, jnp.maximum(j * (rows // HALO_ROWS) - 1, 0), 0))
    return pl.pallas_call(
        _prompt_post_kernel,
        grid=(b, t // rows),
        in_specs=[row_spec, halo_spec] + [row_spec] * 6
        + [pl.BlockSpec((1, 1, d), lambda i, j: (i, 0, 0))] + _weight_specs(d, gdim, None),
        out_specs=row_spec,
        out_shape=jax.ShapeDtypeStruct((b, t, d), F32),
        scratch_shapes=[pltpu.VMEM((HALO_ROWS + rows, d), F32)],
        compiler_params=pltpu.CompilerParams(
            dimension_semantics=("parallel", "parallel"), vmem_limit_bytes=VMEM_LIMIT),
        name="prompt_post",
    )(u, u, pg, o_n, ag, mgp, mga, x, gate, *weights)


def _sample_post(hist, u, pg, o_n, ag, mgp, mga, x, gate, weights, start_pos):
    n_tok, bd, d = u.shape
    n = n_tok * bd
    gdim = d // len(POOL_WINDOWS)
    flat = pl.BlockSpec((n, d), lambda i: (0, 0))
    return pl.pallas_call(
        functools.partial(_sample_post_kernel, start_pos=start_pos),
        grid=(1,),
        in_specs=[pl.BlockSpec(hist.shape, lambda i: (0, 0, 0)), pl.BlockSpec(u.shape, lambda i: (0, 0, 0))]
        + [flat] * 7 + _weight_specs(d, gdim, None),
        out_specs=flat,
        out_shape=jax.ShapeDtypeStruct((n, d), F32),
        compiler_params=pltpu.CompilerParams(vmem_limit_bytes=VMEM_LIMIT),
        name="sample_post",
    )(hist, u, pg, o_n, ag, mgp, mga, x, gate, *weights)


def _token_major(a):
    return jnp.swapaxes(a, 0, 1)


def kernel(x_prompt, x_sample, cache_k, cache_v, state_pool, page_table, c_prompt, c_sample, rel_bias, w_ada, b_ada, g_pre, g_post, w_in, w_pool_grp, pool_scale, w_proj_pool, lambda_q1, lambda_k1, lambda_q2, lambda_k2, g_subln, w_proj_attn, w_out):
    depth = w_in.shape[0]
    b, t, d = x_prompt.shape
    bd, ts, _ = x_sample.shape
    n_pages = page_table.shape[1]
    n_phys, page = cache_k.shape[1], cache_k.shape[2]
    past_len = n_pages * page
    assert ts == 4 and d == N_HEADS * V_DIM

    tile = min(ATTN_TILE, t)
    ar = jnp.arange(tile, dtype=jnp.int32)
    n_prompt = jnp.stack([ar[None, :] - ar[:, None], tile + ar[None, :] - ar[:, None]])
    bias_prompt = _bias_tiles(rel_bias, n_prompt.reshape(2 * tile, tile)).reshape(N_HEADS, 2, tile, tile)

    tq = jnp.arange(8, dtype=jnp.int32)[:, None] % ts
    tk = jnp.arange(page, dtype=jnp.int32)[None, :]
    n_new = jnp.where(tk < ts, tq - tk, -1)
    n_sample = jnp.concatenate([page + tq - tk, n_new], axis=1)
    bias_sample = _bias_tiles(rel_bias, n_sample).reshape(_SROWS, 2 * page)
    bias_sample = jnp.concatenate(
        [jnp.repeat(bias_sample[:, :page], N_HEADS, axis=1), bias_sample[:, page:]], axis=1)

    xp = x_prompt
    xs = _token_major(x_sample).reshape(1, ts * bd, d)
    outs = [[] for _ in range(6)]
    for l in range(depth):
        lam_init = 0.8 - 0.6 * math.exp(-0.3 * l)
        lam_rows = jnp.stack([lambda_q1[l], lambda_k1[l], lambda_q2[l], lambda_k2[l]])
        w_in_l = w_in[l].astype(BF16)
        weights = (w_pool_grp[l].astype(BF16), pool_scale[l].reshape(1, d), w_proj_pool[l].astype(BF16),
                   w_proj_attn[l].astype(BF16), w_out[l].astype(BF16), g_post[l].reshape(1, d))

        mod = _modulation(jnp.concatenate([c_prompt, c_sample]), w_ada[l], b_ada[l])
        shift, scale, gate = mod[:, :d], mod[:, d:2 * d], mod[:, 2 * d:]

        u, pg, q, k, v, ag, mgp, mga, kb, vt = _in_projection(
            xp, scale[:b, None], shift[:b, None], g_pre[l], w_in_l, min(IN_ROWS, t), True)
        tile_rows = lambda a: jnp.tile(a, (ts, 1))[None]
        us, pgs, qs, ks, vs, ags, mgps, mgas = _in_projection(
            xs, tile_rows(scale[b:]), tile_rows(shift[b:]), g_pre[l], w_in_l, ts * bd, False)

        seq_major = lambda a: _token_major(a.reshape(ts, bd, d))
        pad8 = lambda a: jnp.concatenate([a, jnp.zeros_like(a)], axis=1)
        q_s = seq_major(qs)
        o_n, o8 = _attention(
            page_table, lam_rows, g_subln[l], q, kb, vt, bias_prompt,
            jnp.concatenate([q_s, q_s], axis=1), pad8(seq_major(ks)), pad8(seq_major(vs)), bias_sample,
            cache_k[l].reshape(n_phys, page * N_HEADS, V_DIM),
            cache_v[l].reshape(n_phys, page * N_HEADS, V_DIM), lam_init)

        xp = _prompt_post(u, pg, o_n, ag, mgp, mga, xp, gate[:b, None], weights)
        outs[0].append(k.reshape(b, t, N_HEADS, V_DIM))
        outs[1].append(v.reshape(b, t, N_HEADS, V_DIM))
        outs[2].append(u[:, t - POOL_BUF:])

        o_ns = _token_major(o8[:, :ts]).reshape(ts * bd, d)
        hist = _token_major(state_pool[l])
        flat = lambda a: a.reshape(ts * bd, d)
        xs = _sample_post(hist, us.reshape(ts, bd, d), flat(pgs), o_ns, flat(ags), flat(mgps), flat(mgas),
                          flat(xs), jnp.tile(gate[b:], (ts, 1)), weights, past_len).reshape(1, ts * bd, d)
        outs[3].append(seq_major(ks).reshape(bd, ts, N_HEADS, V_DIM))
        outs[4].append(seq_major(vs).reshape(bd, ts, N_HEADS, V_DIM))
        outs[5].append(jnp.concatenate([state_pool[l], seq_major(us)], axis=1)[:, ts:])

    y_sample = _token_major(xs.reshape(ts, bd, d))
    return (xp, y_sample) + tuple(jnp.stack(o) for o in outs)
```

```python
import functools
import math

import jax
import jax.numpy as jnp
from jax import lax
from jax.experimental import pallas as pl
from jax.experimental.pallas import tpu as pltpu

N_HEADS = 8
HEAD_DIM = 64
V_DIM = 2 * HEAD_DIM
POOL_WINDOWS = (2, 4, 8, 16)
POOL_BUF = max(POOL_WINDOWS) - 1
NUM_BUCKETS = 32
MAX_EXACT = NUM_BUCKETS // 2
MAX_DISTANCE = 128
RMS_EPS = 1e-6
NEG_INF = -1e30
ATTN_SCALE = HEAD_DIM ** -0.5
LOG2E = math.log2(math.e)

BF16 = jnp.bfloat16
F32 = jnp.float32

V7X_VMEM_BYTES = 64 * 1024 * 1024
VMEM_LIMIT = V7X_VMEM_BYTES - 8 * 1024 * 1024

IN_ROWS = 512
ATTN_TILE = 512
POST_ROWS = 512
HALO_ROWS = 32


def _silu(x):
    return x * (1.0 / (1.0 + jnp.exp(-x)))


def _sigmoid(x):
    return 1.0 / (1.0 + jnp.exp(-x))


def _rms(x, g):
    return x * lax.rsqrt(jnp.mean(x * x, axis=-1, keepdims=True) + RMS_EPS) * g


def _mod_kernel(c_ref, w_ref, b_ref, o_ref):
    a = _silu(c_ref[...]).astype(BF16)
    o_ref[...] = jnp.dot(a, w_ref[...].astype(BF16), preferred_element_type=F32) + b_ref[...]


def _modulation(c, w_ada, b_ada):
    n, d = c.shape
    return pl.pallas_call(
        _mod_kernel,
        grid=(3,),
        in_specs=[
            pl.BlockSpec((n, d), lambda j: (0, 0)),
            pl.BlockSpec((d, d), lambda j: (0, j)),
            pl.BlockSpec((1, d), lambda j: (0, j)),
        ],
        out_specs=pl.BlockSpec((n, d), lambda j: (0, j)),
        out_shape=jax.ShapeDtypeStruct((n, 3 * d), F32),
        name="modulation",
    )(c, w_ada, b_ada.reshape(1, 3 * d))


_IN_GROUPS = 8


def _in_kernel(x_ref, scale_ref, shift_ref, g_ref, w_ref,
               u_ref, pg_ref, q_ref, k_ref, v_ref, ag_ref, mgp_ref, mga_ref, *attn_refs):
    d = x_ref.shape[-1]
    x = x_ref[0]
    h = (_rms(x, g_ref[...]) * (1.0 + scale_ref[0]) + shift_ref[0]).astype(BF16)

    def z(c):
        return jnp.dot(h, w_ref[:, c * d:(c + 1) * d], preferred_element_type=F32)

    u_ref[0] = z(0)
    pg_ref[0] = _silu(z(1)).astype(BF16)
    q_ref[0] = (z(2) * (ATTN_SCALE * LOG2E)).astype(BF16)
    zk, zv = z(3), z(4)
    k_ref[0] = zk
    v_ref[0] = zv
    if attn_refs:
        attn_refs[0][0] = zk.astype(BF16)
        attn_refs[1][0] = zv.T.astype(BF16)
    ag_ref[0] = _silu(z(5)).astype(BF16)
    mgp_ref[0] = _sigmoid(z(6)).astype(BF16)
    mga_ref[0] = _sigmoid(z(7)).astype(BF16)


def _in_projection(x, scale, shift, g_pre, w_in_bf16, rows, attn_layouts):
    b, t, d = x.shape
    mod_rows = rows if scale.shape[1] == t else 1
    mod_map = (lambda i, j: (i, j, 0)) if scale.shape[1] == t else (lambda i, j: (i, 0, 0))
    row_spec = pl.BlockSpec((1, rows, d), lambda i, j: (i, j, 0))
    out_dtypes = (F32, BF16, BF16, F32, F32, BF16, BF16, BF16)
    out_specs = [row_spec] * _IN_GROUPS
    out_shape = [jax.ShapeDtypeStruct((b, t, d), dt) for dt in out_dtypes]
    if attn_layouts:
        out_specs = out_specs + [row_spec, pl.BlockSpec((1, d, rows), lambda i, j: (i, 0, j))]
        out_shape = out_shape + [jax.ShapeDtypeStruct((b, t, d), BF16), jax.ShapeDtypeStruct((b, d, t), BF16)]
    return pl.pallas_call(
        _in_kernel,
        grid=(b, t // rows),
        in_specs=[
            row_spec,
            pl.BlockSpec((1, mod_rows, d), mod_map),
            pl.BlockSpec((1, mod_rows, d), mod_map),
            pl.BlockSpec((1, d), lambda i, j: (0, 0)),
            pl.BlockSpec((d, _IN_GROUPS * d), lambda i, j: (0, 0), pipeline_mode=pl.Buffered(1)),
        ],
        out_specs=out_specs,
        out_shape=out_shape,
        compiler_params=pltpu.CompilerParams(
            dimension_semantics=("parallel", "parallel"), vmem_limit_bytes=VMEM_LIMIT),
        name="in_projection",
    )(x, scale, shift, g_pre.reshape(1, d), w_in_bf16)


def _bias_kernel(tab_ref, n_ref, o_ref):
    h = pl.program_id(0)
    n = n_ref[...]
    nc = jnp.maximum(n, 0)
    nf = jnp.maximum(nc, 1).astype(F32)
    large = MAX_EXACT + (jnp.log(nf / MAX_EXACT) / math.log(MAX_DISTANCE / MAX_EXACT)
                         * (NUM_BUCKETS - MAX_EXACT)).astype(jnp.int32)
    large = jnp.minimum(large, NUM_BUCKETS - 1)
    bucket = jnp.where(nc < MAX_EXACT, nc, large)
    far = tab_ref[h, NUM_BUCKETS - 1]
    acc = jnp.zeros(n.shape, F32)
    for b in range(NUM_BUCKETS):
        acc = jnp.where(bucket == b, (tab_ref[h, b] - far) * LOG2E, acc)
    o_ref[0] = jnp.where(n >= 0, acc, NEG_INF)


def _bias_tiles(rel_bias, n):
    r, c = n.shape
    return pl.pallas_call(
        _bias_kernel,
        grid=(N_HEADS,),
        in_specs=[
            pl.BlockSpec(memory_space=pltpu.SMEM),
            pl.BlockSpec((r, c), lambda h: (0, 0)),
        ],
        out_specs=pl.BlockSpec((1, r, c), lambda h: (h, 0, 0)),
        out_shape=jax.ShapeDtypeStruct((N_HEADS, r, c), F32),
        name="bias_tiles",
    )(rel_bias.T, n)


def _lambda(lam_ref, lam_init):
    e1 = jnp.exp(jnp.sum(lam_ref[0:1, :] * lam_ref[1:2, :], axis=-1, keepdims=True))
    e2 = jnp.exp(jnp.sum(lam_ref[2:3, :] * lam_ref[3:4, :], axis=-1, keepdims=True))
    return e1 - e2 + lam_init


def _prompt_attn_body(qi, lam_ref, g_ref, q_ref, kb_ref, vt_ref, bias_ref, o_ref,
                      s_ref, acc_ref, ml_ref, lam_init, side_work):
    tile = q_ref.shape[1]

    qt = q_ref[0].astype(F32).T
    first = lax.broadcasted_iota(jnp.int32, qt.shape, 0) < HEAD_DIM
    qts = (jnp.where(first, qt, 0.0).astype(BF16), jnp.where(first, 0.0, qt).astype(BF16))

    ml_ref[0:2, :] = jnp.full((2, tile), NEG_INF, F32)
    ml_ref[2:4, :] = jnp.zeros((2, tile), F32)
    acc_ref[...] = jnp.zeros(acc_ref.shape, F32)

    def key_rows(back):
        j = jnp.maximum(qi - back, 0)
        return pl.ds(pl.multiple_of(j * tile, tile), tile)

    def score(back, slot):
        kt = kb_ref[0, key_rows(back), :]
        for m in range(2):
            s_ref[slot, m] = jnp.dot(kt, qts[m], preferred_element_type=F32)

    def consume(back, slot, bias):
        vt = vt_ref[0, :, key_rows(back)]
        for m in range(2):
            s = s_ref[slot, m]
            if bias is not None:
                s = s + bias
            m_old = ml_ref[m:m + 1, :]
            m_new = jnp.maximum(m_old, jnp.max(s, axis=0, keepdims=True))
            p = jnp.exp2(s - m_new)
            corr = jnp.exp2(m_old - m_new)
            ml_ref[m:m + 1, :] = m_new
            ml_ref[2 + m:3 + m, :] = ml_ref[2 + m:3 + m, :] * corr + jnp.sum(p, axis=0, keepdims=True)
            acc_ref[m] = acc_ref[m] * corr + jnp.dot(vt, p.astype(BF16), preferred_element_type=F32)

    score(0, 0)
    score(1, 1)
    consume(0, 0, bias_ref[0, 0])
    side_work()

    @pl.when(qi >= 1)
    def _():
        score(2, 0)
        consume(1, 1, bias_ref[0, 1])

    n_far = jnp.maximum(qi - 1, 0)

    def far_pair(p, carry):
        back = 2 * p + 2
        score(back + 1, 1)
        consume(back, 0, None)
        score(back + 2, 0)
        consume(back + 1, 1, None)
        return carry

    lax.fori_loop(0, n_far // 2, far_pair, 0)

    @pl.when(n_far % 2 == 1)
    def _():
        consume(qi, 0, None)

    lam = _lambda(lam_ref, lam_init)
    inv_l = 1.0 / ml_ref[2:4, :]
    o = acc_ref[0] * inv_l[0:1, :] - lam * (acc_ref[1] * inv_l[1:2, :])
    o_ref[0] = (_rms(o.T, g_ref[...]) * (1.0 - lam_init)).astype(o_ref.dtype)


_SROWS = N_HEADS * 8


def _sample_attn_steps(c, last, lam_ref, g_ref, q_ref, kn_ref, vn_ref, bias_ref, k_refs, v_refs, o_ref,
                       wq_ref, wn_ref, mask_ref, acc_ref, m_ref, l_ref, lam_init):
    pages = len(k_refs)
    page_rows = k_refs[0].shape[1]
    page = page_rows // N_HEADS
    width = q_ref.shape[-1]
    nt = (((1,), (1,)), ((), ()))

    def start_body():
        q8 = q_ref[0]
        qh = jnp.concatenate([q8[:, h * V_DIM:(h + 1) * V_DIM] for h in range(N_HEADS)], axis=0)
        row = lax.broadcasted_iota(jnp.int32, qh.shape, 0)
        lane = lax.broadcasted_iota(jnp.int32, qh.shape, 1)
        wq_ref[...] = jnp.where(lane // HEAD_DIM == (row % 8) // 4, qh, 0).astype(BF16)
        qrep = jnp.concatenate([q8] * N_HEADS, axis=0)
        row = lax.broadcasted_iota(jnp.int32, qrep.shape, 0)
        col = lax.broadcasted_iota(jnp.int32, qrep.shape, 1)
        wn_ref[...] = jnp.where(col // HEAD_DIM == row // 4, qrep, 0).astype(BF16)
        row = lax.broadcasted_iota(jnp.int32, mask_ref.shape, 0)
        col = lax.broadcasted_iota(jnp.int32, mask_ref.shape, 1)
        mask_ref[...] = jnp.where(col % N_HEADS == row // 8, 0.0, NEG_INF)
        m_ref[...] = jnp.full(m_ref.shape, NEG_INF, F32)
        l_ref[...] = jnp.zeros(l_ref.shape, F32)
        acc_ref[...] = jnp.zeros(acc_ref.shape, F32)

    def update(s, values, own_head_blocks):
        m_old = m_ref[...]
        m_new = jnp.maximum(m_old, jnp.max(s, axis=-1, keepdims=True))
        p = jnp.exp2(s - m_new)
        corr = jnp.exp2(m_old - m_new)
        m_ref[...] = m_new
        l_ref[...] = l_ref[...] * corr + jnp.sum(p, axis=-1, keepdims=True)
        pb = p.astype(BF16)
        pv, off = None, 0
        for val in values:
            n = val.shape[0]
            part = jnp.dot(pb[:, off:off + n], val, preferred_element_type=F32)
            pv = part if pv is None else pv + part
            off += n
        if own_head_blocks:
            pv = jnp.concatenate(
                [pv[h * 8:(h + 1) * 8, h * V_DIM:(h + 1) * V_DIM] for h in range(N_HEADS)], axis=0)
        acc_ref[...] = acc_ref[...] * corr + pv

    def pages_step():
        wq = wq_ref[...]
        group = min(pages, 16)
        for g0 in range(0, pages, group):
            parts = []
            for j in range(g0, g0 + group):
                sj = lax.dot_general(wq, k_refs[j][0].astype(BF16), nt, preferred_element_type=F32)
                sj = sj + mask_ref[...]
                if j == pages - 1:
                    sj = sj + jnp.where(c == last, bias_ref[:, :page_rows], 0.0)
                parts.append(sj)
            update(jnp.concatenate(parts, axis=-1),
                   [v_refs[j][0].astype(BF16) for j in range(g0, g0 + group)], False)

    def finish_body():
        pad = jnp.zeros((page - 8, width), F32)
        kn = jnp.concatenate([kn_ref[0], pad], axis=0).astype(BF16)
        vn = jnp.concatenate([vn_ref[0], pad], axis=0).astype(BF16)
        s_new = lax.dot_general(wn_ref[...], kn, nt, preferred_element_type=F32)
        update(s_new + bias_ref[:, page_rows:], [vn], True)
        lam = _lambda(lam_ref, lam_init)
        d = acc_ref[...] / l_ref[...]
        o = d - lam * pltpu.roll(d, _SROWS - 4, 0)
        o = _rms(o, g_ref[...]) * (1.0 - lam_init)
        for h in range(N_HEADS):
            o_ref[0, :, h * V_DIM:(h + 1) * V_DIM] = o[h * 8:(h + 1) * 8]

    return (lambda: pl.when(c == 0)(start_body)), pages_step, (lambda: pl.when(c == last)(finish_body))


def _attn_kernel(pt_ref, lam_ref, g_ref, q_ref, kb_ref, vt_ref, biasp_ref, qs_ref, kn_ref, vn_ref,
                 biass_ref, *rest, lam_init, pages, chunks):
    del pt_ref
    k_refs = rest[:pages]
    v_refs = rest[pages:2 * pages]
    o_ref, os_ref = rest[2 * pages:2 * pages + 2]
    s_ref, acc_ref, ml_ref, wq_ref, wn_ref, mask_ref, accs_ref, ms_ref, ls_ref = rest[2 * pages + 2:]
    step = (pl.program_id(0) * pl.num_programs(1) + pl.program_id(1)) * pl.num_programs(2) + pl.program_id(2)
    sample_start, sample_pages, sample_finish = _sample_attn_steps(
        step % chunks, chunks - 1, lam_ref, g_ref, qs_ref, kn_ref, vn_ref, biass_ref,
        k_refs, v_refs, os_ref, wq_ref, wn_ref, mask_ref, accs_ref, ms_ref, ls_ref, lam_init)
    sample_start()
    _prompt_attn_body(pl.program_id(2), lam_ref, g_ref, q_ref, kb_ref, vt_ref, biasp_ref, o_ref,
                      s_ref, acc_ref, ml_ref, lam_init, sample_pages)
    sample_finish()


def _attention(page_table, lam_rows, g_subln, q, kb, vt, bias_prompt, q8, kn8, vn8, bias_sample,
               cache_k, cache_v, lam_init):
    b, t, d = q.shape
    bd, n_pages = page_table.shape
    page_rows, vdim = cache_k.shape[1:]
    tile = min(ATTN_TILE, t)
    nq = t // tile
    assert t % tile == 0 and tile >= MAX_DISTANCE and bias_prompt.shape == (N_HEADS, 2, tile, tile)
    steps = N_HEADS * b * nq
    assert (bd * n_pages) % steps == 0
    pages = bd * n_pages // steps
    assert n_pages % pages == 0 and (pages <= 8 or pages % 8 == 0)
    chunks = n_pages // pages

    def seq_of(h, i, j):
        return ((h * b + i) * nq + j) // chunks

    def page_spec(p):
        return pl.BlockSpec(
            (1, page_rows, vdim), lambda h, i, j, pt: (pt[((h * b + i) * nq + j) * pages + p], 0, 0))

    small = pl.BlockSpec((1, 8, d), lambda h, i, j, pt: (seq_of(h, i, j), 0, 0))
    grid_spec = pltpu.PrefetchScalarGridSpec(
        num_scalar_prefetch=1,
        grid=(N_HEADS, b, nq),
        in_specs=[
            pl.BlockSpec((4, HEAD_DIM), lambda h, i, j, pt: (0, 0)),
            pl.BlockSpec((1, V_DIM), lambda h, i, j, pt: (0, 0)),
            pl.BlockSpec((1, tile, V_DIM), lambda h, i, j, pt: (i, j, h)),
            pl.BlockSpec((1, t, V_DIM), lambda h, i, j, pt: (i, 0, h)),
            pl.BlockSpec((1, V_DIM, t), lambda h, i, j, pt: (i, h, 0)),
            pl.BlockSpec((1, 2, tile, tile), lambda h, i, j, pt: (h, 0, 0, 0), pipeline_mode=pl.Buffered(1)),
            small, small, small,
            pl.BlockSpec(bias_sample.shape, lambda h, i, j, pt: (0, 0)),
        ] + [page_spec(p) for p in range(pages)] + [page_spec(p) for p in range(pages)],
        out_specs=[
            pl.BlockSpec((1, tile, V_DIM), lambda h, i, j, pt: (i, j, h)),
            pl.BlockSpec((1, 8, d), lambda h, i, j, pt: (seq_of(h, i, j), 0, 0)),
        ],
        scratch_shapes=[
            pltpu.VMEM((2, 2, tile, tile), F32),
            pltpu.VMEM((2, V_DIM, tile), F32),
            pltpu.VMEM((8, tile), F32),
            pltpu.VMEM((_SROWS, V_DIM), BF16),
            pltpu.VMEM((_SROWS, d), BF16),
            pltpu.VMEM((_SROWS, page_rows), F32),
            pltpu.VMEM((_SROWS, V_DIM), F32),
            pltpu.VMEM((_SROWS, 1), F32),
            pltpu.VMEM((_SROWS, 1), F32),
        ],
    )
    return pl.pallas_call(
        functools.partial(_attn_kernel, lam_init=lam_init, pages=pages, chunks=chunks),
        grid_spec=grid_spec,
        out_shape=[jax.ShapeDtypeStruct((b, t, d), BF16), jax.ShapeDtypeStruct((bd, 8, d), F32)],
        compiler_params=pltpu.CompilerParams(
            dimension_semantics=("arbitrary", "arbitrary", "arbitrary"), vmem_limit_bytes=VMEM_LIMIT),
        name="attention",
    )(page_table.reshape(-1), lam_rows, g_subln.reshape(1, V_DIM), q, kb, vt, bias_prompt, q8, kn8, vn8,
      bias_sample,
      *([cache_k] * pages), *([cache_v] * pages))


def _finish(d_groups, pg, o_n, ag, mgp, mga, x, gate, wg_ref, ps_ref, wpp_ref, wpa_ref, wo_ref, gp_ref):
    y = jnp.concatenate(
        [jnp.dot(d, wg_ref[g], preferred_element_type=F32) for g, d in enumerate(d_groups)], axis=-1)
    y = (y * ps_ref[...] * pg.astype(F32)).astype(BF16)
    y_pool = jnp.dot(y, wpp_ref[...], preferred_element_type=F32)
    a = (o_n.astype(F32) * ag.astype(F32)).astype(BF16)
    y_attn = jnp.dot(a, wpa_ref[...], preferred_element_type=F32)
    merged = (mgp.astype(F32) * y_pool + mga.astype(F32) * y_attn).astype(BF16)
    r = jnp.dot(merged, wo_ref[...], preferred_element_type=F32)
    return x + gate * _rms(r, gp_ref[...])


def _prompt_post_kernel(u_ref, halo_ref, pg_ref, o_ref, ag_ref, mgp_ref, mga_ref, x_ref, gate_ref,
                        wg_ref, ps_ref, wpp_ref, wpa_ref, wo_ref, gp_ref, y_ref, ext_ref, sum_a_ref, sum_b_ref):
    i = pl.program_id(1)
    rows = u_ref.shape[1]
    gdim = wg_ref.shape[1]
    d = u_ref.shape[2]
    total = HALO_ROWS + rows
    u = u_ref[0]
    ext_ref[0:HALO_ROWS, :] = jnp.where(i == 0, 0.0, halo_ref[0])
    ext_ref[HALO_ROWS:, :] = u

    def stage(src_ref, dst_ref, shift, first_row, first_col):
        n = total - first_row
        dst_ref[first_row:, first_col:] = (
            src_ref[first_row:, first_col:] + src_ref[first_row - shift:first_row - shift + n, first_col:])

    assert POOL_WINDOWS == (2, 4, 8, 16) and HALO_ROWS == 32 and d == 4 * gdim
    stage(ext_ref, sum_a_ref, 1, 8, 0)
    stage(sum_a_ref, sum_b_ref, 2, 16, gdim)
    stage(sum_b_ref, sum_a_ref, 4, 24, 2 * gdim)
    wide = (sum_a_ref[HALO_ROWS:, 3 * gdim:] + sum_a_ref[HALO_ROWS - 8:total - 8, 3 * gdim:])
    sums = (sum_a_ref[HALO_ROWS:, 0:gdim], sum_b_ref[HALO_ROWS:, gdim:2 * gdim],
            sum_a_ref[HALO_ROWS:, 2 * gdim:3 * gdim], wide)

    pos = i * rows + lax.broadcasted_iota(jnp.int32, (rows, 1), 0)
    d_groups = []
    for g, w in enumerate(POOL_WINDOWS):
        inv_cnt = 1.0 / jnp.minimum(pos + 1, w).astype(F32)
        d_groups.append((sums[g] * inv_cnt - u[:, g * gdim:(g + 1) * gdim]).astype(BF16))
    y_ref[0] = _finish(d_groups, pg_ref[0], o_ref[0], ag_ref[0], mgp_ref[0], mga_ref[0], x_ref[0],
                       gate_ref[0], wg_ref, ps_ref, wpp_ref, wpa_ref, wo_ref, gp_ref)


def _sample_post_kernel(hist_ref, u_ref, pg_ref, o_ref, ag_ref, mgp_ref, mga_ref, x_ref, gate_ref,
                        wg_ref, ps_ref, wpp_ref, wpa_ref, wo_ref, gp_ref, y_ref, *, start_pos):
    n_tok = u_ref.shape[0]
    gdim = wg_ref.shape[1]
    ext = [hist_ref[j] for j in range(POOL_BUF)] + [u_ref[t] for t in range(n_tok)]
    d_groups = []
    for g, w in enumerate(POOL_WINDOWS):
        cols = slice(g * gdim, (g + 1) * gdim)
        per_tok = []
        for t in range(n_tok):
            s = ext[POOL_BUF + t][:, cols]
            for back in range(1, w):
                s = s + ext[POOL_BUF + t - back][:, cols]
            cnt = float(min(start_pos + t + 1, w))
            per_tok.append((s / cnt - ext[POOL_BUF + t][:, cols]).astype(BF16))
        d_groups.append(jnp.concatenate(per_tok, axis=0))
    y_ref[...] = _finish(d_groups, pg_ref[...], o_ref[...], ag_ref[...], mgp_ref[...], mga_ref[...],
                         x_ref[...], gate_ref[...], wg_ref, ps_ref, wpp_ref, wpa_ref, wo_ref, gp_ref)


def _weight_specs(d, gdim, index):
    return [
        pl.BlockSpec((len(POOL_WINDOWS), gdim, gdim), lambda *a: (0, 0, 0)),
        pl.BlockSpec((1, d), lambda *a: (0, 0)),
        pl.BlockSpec((d, d), lambda *a: (0, 0)),
        pl.BlockSpec((d, d), lambda *a: (0, 0)),
        pl.BlockSpec((d, d), lambda *a: (0, 0)),
        pl.BlockSpec((1, d), lambda *a: (0, 0)),
    ]


def _prompt_post(u, pg, o_n, ag, mgp, mga, x, gate, weights):
    b, t, d = x.shape
    rows = min(POST_ROWS, t)
    gdim = d // len(POOL_WINDOWS)
    row_spec = pl.BlockSpec((1, rows, d), lambda i, j: (i, j, 0))
    halo_spec = pl.BlockSpec(
        (1, HALO_ROWS, d), lambda i, j: (i, jnp.maximum(j * (rows // HALO_ROWS) - 1, 0), 0))
    return pl.pallas_call(
        _prompt_post_kernel,
        grid=(b, t // rows),
        in_specs=[row_spec, halo_spec] + [row_spec] * 6
        + [pl.BlockSpec((1, 1, d), lambda i, j: (i, 0, 0))] + _weight_specs(d, gdim, None),
        out_specs=row_spec,
        out_shape=jax.ShapeDtypeStruct((b, t, d), F32),
        scratch_shapes=[pltpu.VMEM((HALO_ROWS + rows, d), F32)] * 3,
        compiler_params=pltpu.CompilerParams(
            dimension_semantics=("parallel", "parallel"), vmem_limit_bytes=VMEM_LIMIT),
        name="prompt_post",
    )(u, u, pg, o_n, ag, mgp, mga, x, gate, *weights)


def _sample_post(hist, u, pg, o_n, ag, mgp, mga, x, gate, weights, start_pos):
    n_tok, bd, d = u.shape
    n = n_tok * bd
    gdim = d // len(POOL_WINDOWS)
    flat = pl.BlockSpec((n, d), lambda i: (0, 0))
    return pl.pallas_call(
        functools.partial(_sample_post_kernel, start_pos=start_pos),
        grid=(1,),
        in_specs=[pl.BlockSpec(hist.shape, lambda i: (0, 0, 0)), pl.BlockSpec(u.shape, lambda i: (0, 0, 0))]
        + [flat] * 7 + _weight_specs(d, gdim, None),
        out_specs=flat,
        out_shape=jax.ShapeDtypeStruct((n, d), F32),
        compiler_params=pltpu.CompilerParams(vmem_limit_bytes=VMEM_LIMIT),
        name="sample_post",
    )(hist, u, pg, o_n, ag, mgp, mga, x, gate, *weights)


def _token_major(a):
    return jnp.swapaxes(a, 0, 1)


def kernel(x_prompt, x_sample, cache_k, cache_v, state_pool, page_table, c_prompt, c_sample, rel_bias, w_ada, b_ada, g_pre, g_post, w_in, w_pool_grp, pool_scale, w_proj_pool, lambda_q1, lambda_k1, lambda_q2, lambda_k2, g_subln, w_proj_attn, w_out):
    depth = w_in.shape[0]
    b, t, d = x_prompt.shape
    bd, ts, _ = x_sample.shape
    n_pages = page_table.shape[1]
    n_phys, page = cache_k.shape[1], cache_k.shape[2]
    past_len = n_pages * page
    assert ts == 4 and d == N_HEADS * V_DIM

    tile = min(ATTN_TILE, t)
    ar = jnp.arange(tile, dtype=jnp.int32)
    n_prompt = jnp.stack([ar[None, :] - ar[:, None], tile + ar[None, :] - ar[:, None]])
    bias_prompt = _bias_tiles(rel_bias, n_prompt.reshape(2 * tile, tile)).reshape(N_HEADS, 2, tile, tile)

    tq = jnp.arange(8, dtype=jnp.int32)[:, None] % ts
    tk = jnp.arange(page, dtype=jnp.int32)[None, :]
    n_new = jnp.where(tk < ts, tq - tk, -1)
    n_sample = jnp.concatenate([page + tq - tk, n_new], axis=1)
    bias_sample = _bias_tiles(rel_bias, n_sample).reshape(_SROWS, 2 * page)
    bias_sample = jnp.concatenate(
        [jnp.repeat(bias_sample[:, :page], N_HEADS, axis=1), bias_sample[:, page:]], axis=1)

    xp = x_prompt
    xs = _token_major(x_sample).reshape(1, ts * bd, d)
    outs = [[] for _ in range(6)]
    for l in range(depth):
        lam_init = 0.8 - 0.6 * math.exp(-0.3 * l)
        lam_rows = jnp.stack([lambda_q1[l], lambda_k1[l], lambda_q2[l], lambda_k2[l]])
        w_in_l = w_in[l].astype(BF16)
        weights = (w_pool_grp[l].astype(BF16), pool_scale[l].reshape(1, d), w_proj_pool[l].astype(BF16),
                   w_proj_attn[l].astype(BF16), w_out[l].astype(BF16), g_post[l].reshape(1, d))

        mod = _modulation(jnp.concatenate([c_prompt, c_sample]), w_ada[l], b_ada[l])
        shift, scale, gate = mod[:, :d], mod[:, d:2 * d], mod[:, 2 * d:]

        u, pg, q, k, v, ag, mgp, mga, kb, vt = _in_projection(
            xp, scale[:b, None], shift[:b, None], g_pre[l], w_in_l, min(IN_ROWS, t), True)
        tile_rows = lambda a: jnp.tile(a, (ts, 1))[None]
        us, pgs, qs, ks, vs, ags, mgps, mgas = _in_projection(
            xs, tile_rows(scale[b:]), tile_rows(shift[b:]), g_pre[l], w_in_l, ts * bd, False)

        seq_major = lambda a: _token_major(a.reshape(ts, bd, d))
        pad8 = lambda a: jnp.concatenate([a, jnp.zeros_like(a)], axis=1)
        q_s = seq_major(qs)
        o_n, o8 = _attention(
            page_table, lam_rows, g_subln[l], q, kb, vt, bias_prompt,
            jnp.concatenate([q_s, q_s], axis=1), pad8(seq_major(ks)), pad8(seq_major(vs)), bias_sample,
            cache_k[l].reshape(n_phys, page * N_HEADS, V_DIM),
            cache_v[l].reshape(n_phys, page * N_HEADS, V_DIM), lam_init)

        xp = _prompt_post(u, pg, o_n, ag, mgp, mga, xp, gate[:b, None], weights)
        outs[0].append(k.reshape(b, t, N_HEADS, V_DIM))
        outs[1].append(v.reshape(b, t, N_HEADS, V_DIM))
        outs[2].append(u[:, t - POOL_BUF:])

        o_ns = _token_major(o8[:, :ts]).reshape(ts * bd, d)
        hist = _token_major(state_pool[l])
        flat = lambda a: a.reshape(ts * bd, d)
        xs = _sample_post(hist, us.reshape(ts, bd, d), flat(pgs), o_ns, flat(ags), flat(mgps), flat(mgas),
                          flat(xs), jnp.tile(gate[b:], (ts, 1)), weights, past_len).reshape(1, ts * bd, d)
        outs[3].append(seq_major(ks).reshape(bd, ts, N_HEADS, V_DIM))
        outs[4].append(seq_major(vs).reshape(bd, ts, N_HEADS, V_DIM))
        outs[5].append(jnp.concatenate([state_pool[l], seq_major(us)], axis=1)[:, ts:])

    y_sample = _token_major(xs.reshape(ts, bd, d))
    return (xp, y_sample) + tuple(jnp.stack(o) for o in outs)
```

```python
import functools
import math

import jax
import jax.numpy as jnp
from jax import lax
from jax.experimental import pallas as pl
from jax.experimental.pallas import tpu as pltpu

N_HEADS = 8
HEAD_DIM = 64
V_DIM = 2 * HEAD_DIM
POOL_WINDOWS = (2, 4, 8, 16)
POOL_BUF = max(POOL_WINDOWS) - 1
NUM_BUCKETS = 32
MAX_EXACT = NUM_BUCKETS // 2
MAX_DISTANCE = 128
RMS_EPS = 1e-6
NEG_INF = -1e30
ATTN_SCALE = HEAD_DIM ** -0.5
LOG2E = math.log2(math.e)

BF16 = jnp.bfloat16
F32 = jnp.float32

V7X_VMEM_BYTES = 64 * 1024 * 1024
VMEM_LIMIT = V7X_VMEM_BYTES - 8 * 1024 * 1024

IN_ROWS = 512
ATTN_TILE = 512
POST_ROWS = 512
HALO_ROWS = 32


def _silu(x):
    return x * (1.0 / (1.0 + jnp.exp(-x)))


def _sigmoid(x):
    return 1.0 / (1.0 + jnp.exp(-x))


def _rms(x, g):
    return x * lax.rsqrt(jnp.mean(x * x, axis=-1, keepdims=True) + RMS_EPS) * g


def _mod_kernel(c_ref, w_ref, b_ref, o_ref):
    a = _silu(c_ref[...]).astype(BF16)
    o_ref[...] = jnp.dot(a, w_ref[...].astype(BF16), preferred_element_type=F32) + b_ref[...]


def _modulation(c, w_ada, b_ada):
    n, d = c.shape
    return pl.pallas_call(
        _mod_kernel,
        grid=(3,),
        in_specs=[
            pl.BlockSpec((n, d), lambda j: (0, 0)),
            pl.BlockSpec((d, d), lambda j: (0, j)),
            pl.BlockSpec((1, d), lambda j: (0, j)),
        ],
        out_specs=pl.BlockSpec((n, d), lambda j: (0, j)),
        out_shape=jax.ShapeDtypeStruct((n, 3 * d), F32),
        name="modulation",
    )(c, w_ada, b_ada.reshape(1, 3 * d))


_IN_GROUPS = 8


def _in_kernel(x_ref, scale_ref, shift_ref, g_ref, w_ref,
               u_ref, pg_ref, q_ref, k_ref, v_ref, ag_ref, mgp_ref, mga_ref, *attn_refs):
    d = x_ref.shape[-1]
    x = x_ref[0]
    h = (_rms(x, g_ref[...]) * (1.0 + scale_ref[0]) + shift_ref[0]).astype(BF16)

    def z(c):
        return jnp.dot(h, w_ref[:, c * d:(c + 1) * d], preferred_element_type=F32)

    u_ref[0] = z(0)
    pg_ref[0] = _silu(z(1)).astype(BF16)
    q_ref[0] = (z(2) * (ATTN_SCALE * LOG2E)).astype(BF16)
    zk, zv = z(3), z(4)
    k_ref[0] = zk
    v_ref[0] = zv
    if attn_refs:
        attn_refs[0][0] = zk.astype(BF16)
        attn_refs[1][0] = zv.T.astype(BF16)
    ag_ref[0] = _silu(z(5)).astype(BF16)
    mgp_ref[0] = _sigmoid(z(6)).astype(BF16)
    mga_ref[0] = _sigmoid(z(7)).astype(BF16)


def _in_projection(x, scale, shift, g_pre, w_in_bf16, rows, attn_layouts):
    b, t, d = x.shape
    mod_rows = rows if scale.shape[1] == t else 1
    mod_map = (lambda i, j: (i, j, 0)) if scale.shape[1] == t else (lambda i, j: (i, 0, 0))
    row_spec = pl.BlockSpec((1, rows, d), lambda i, j: (i, j, 0))
    out_dtypes = (F32, BF16, BF16, F32, F32, BF16, BF16, BF16)
    out_specs = [row_spec] * _IN_GROUPS
    out_shape = [jax.ShapeDtypeStruct((b, t, d), dt) for dt in out_dtypes]
    if attn_layouts:
        out_specs = out_specs + [row_spec, pl.BlockSpec((1, d, rows), lambda i, j: (i, 0, j))]
        out_shape = out_shape + [jax.ShapeDtypeStruct((b, t, d), BF16), jax.ShapeDtypeStruct((b, d, t), BF16)]
    return pl.pallas_call(
        _in_kernel,
        grid=(b, t // rows),
        in_specs=[
            row_spec,
            pl.BlockSpec((1, mod_rows, d), mod_map),
            pl.BlockSpec((1, mod_rows, d), mod_map),
            pl.BlockSpec((1, d), lambda i, j: (0, 0)),
            pl.BlockSpec((d, _IN_GROUPS * d), lambda i, j: (0, 0), pipeline_mode=pl.Buffered(1)),
        ],
        out_specs=out_specs,
        out_shape=out_shape,
        compiler_params=pltpu.CompilerParams(
            dimension_semantics=("parallel", "parallel"), vmem_limit_bytes=VMEM_LIMIT),
        name="in_projection",
    )(x, scale, shift, g_pre.reshape(1, d), w_in_bf16)


def _bias_kernel(tab_ref, n_ref, o_ref):
    h = pl.program_id(0)
    n = n_ref[...]
    nc = jnp.maximum(n, 0)
    nf = jnp.maximum(nc, 1).astype(F32)
    large = MAX_EXACT + (jnp.log(nf / MAX_EXACT) / math.log(MAX_DISTANCE / MAX_EXACT)
                         * (NUM_BUCKETS - MAX_EXACT)).astype(jnp.int32)
    large = jnp.minimum(large, NUM_BUCKETS - 1)
    bucket = jnp.where(nc < MAX_EXACT, nc, large)
    row = tab_ref[pl.ds(h, 1), :]
    row = (row - row[:, NUM_BUCKETS - 1:NUM_BUCKETS]) * LOG2E
    table = jnp.broadcast_to(row, (n.shape[0], row.shape[1]))
    lanes = row.shape[1]
    acc = jnp.concatenate(
        [jnp.take_along_axis(table, bucket[:, c:c + lanes], axis=1) for c in range(0, n.shape[1], lanes)],
        axis=1)
    o_ref[0] = jnp.where(n >= 0, acc, NEG_INF)


def _bias_tiles(rel_bias, n):
    r, c = n.shape
    lanes = 128
    assert c % lanes == 0 and NUM_BUCKETS <= lanes
    table = jnp.pad(rel_bias.T, ((0, 0), (0, lanes - NUM_BUCKETS)))
    return pl.pallas_call(
        _bias_kernel,
        grid=(N_HEADS,),
        in_specs=[
            pl.BlockSpec((N_HEADS, lanes), lambda h: (0, 0)),
            pl.BlockSpec((r, c), lambda h: (0, 0)),
        ],
        out_specs=pl.BlockSpec((1, r, c), lambda h: (h, 0, 0)),
        out_shape=jax.ShapeDtypeStruct((N_HEADS, r, c), F32),
        name="bias_tiles",
    )(table, n)


def _lambda(lam_ref, lam_init):
    e1 = jnp.exp(jnp.sum(lam_ref[0:1, :] * lam_ref[1:2, :], axis=-1, keepdims=True))
    e2 = jnp.exp(jnp.sum(lam_ref[2:3, :] * lam_ref[3:4, :], axis=-1, keepdims=True))
    return e1 - e2 + lam_init


def _prompt_attn_body(qi, lam_ref, g_ref, q_ref, kb_ref, vt_ref, bias_ref, o_ref,
                      s_ref, acc_ref, ml_ref, lam_init, side_work):
    tile = q_ref.shape[1]

    qt = q_ref[0].astype(F32).T
    first = lax.broadcasted_iota(jnp.int32, qt.shape, 0) < HEAD_DIM
    qts = (jnp.where(first, qt, 0.0).astype(BF16), jnp.where(first, 0.0, qt).astype(BF16))

    ml_ref[0:2, :] = jnp.full((2, tile), NEG_INF, F32)
    ml_ref[2:4, :] = jnp.zeros((2, tile), F32)
    acc_ref[...] = jnp.zeros(acc_ref.shape, F32)

    def key_rows(back):
        j = jnp.maximum(qi - back, 0)
        return pl.ds(pl.multiple_of(j * tile, tile), tile)

    def score(back, slot):
        kt = kb_ref[0, key_rows(back), :]
        for m in range(2):
            s_ref[slot, m] = jnp.dot(kt, qts[m], preferred_element_type=F32)

    def consume(back, slot, bias):
        vt = vt_ref[0, :, key_rows(back)]
        for m in range(2):
            s = s_ref[slot, m]
            if bias is not None:
                s = s + bias
            m_old = ml_ref[m:m + 1, :]
            m_new = jnp.maximum(m_old, jnp.max(s, axis=0, keepdims=True))
            p = jnp.exp2(s - m_new)
            corr = jnp.exp2(m_old - m_new)
            ml_ref[m:m + 1, :] = m_new
            ml_ref[2 + m:3 + m, :] = ml_ref[2 + m:3 + m, :] * corr + jnp.sum(p, axis=0, keepdims=True)
            acc_ref[m] = acc_ref[m] * corr + jnp.dot(vt, p.astype(BF16), preferred_element_type=F32)

    score(0, 0)
    score(1, 1)
    consume(0, 0, bias_ref[0, 0])
    side_work()

    @pl.when(qi >= 1)
    def _():
        score(2, 0)
        consume(1, 1, bias_ref[0, 1])

    n_far = jnp.maximum(qi - 1, 0)

    def far_pair(p, carry):
        back = 2 * p + 2
        score(back + 1, 1)
        consume(back, 0, None)
        score(back + 2, 0)
        consume(back + 1, 1, None)
        return carry

    lax.fori_loop(0, n_far // 2, far_pair, 0)

    @pl.when(n_far % 2 == 1)
    def _():
        consume(qi, 0, None)

    lam = _lambda(lam_ref, lam_init)
    inv_l = 1.0 / ml_ref[2:4, :]
    o = acc_ref[0] * inv_l[0:1, :] - lam * (acc_ref[1] * inv_l[1:2, :])
    o = o * lax.rsqrt(jnp.mean(o * o, axis=0, keepdims=True) + RMS_EPS)
    o_ref[0] = (o.T * (g_ref[...] * (1.0 - lam_init))).astype(o_ref.dtype)


_SROWS = N_HEADS * 8


def _sample_attn_steps(c, last, lam_ref, g_ref, q_ref, kn_ref, vn_ref, bias_ref, k_refs, v_refs, o_ref,
                       wq_ref, wn_ref, mask_ref, acc_ref, m_ref, l_ref, lam_init):
    pages = len(k_refs)
    page_rows = k_refs[0].shape[1]
    page = page_rows // N_HEADS
    width = q_ref.shape[-1]
    nt = (((1,), (1,)), ((), ()))

    def start_body():
        q8 = q_ref[0]
        qh = jnp.concatenate([q8[:, h * V_DIM:(h + 1) * V_DIM] for h in range(N_HEADS)], axis=0)
        row = lax.broadcasted_iota(jnp.int32, qh.shape, 0)
        lane = lax.broadcasted_iota(jnp.int32, qh.shape, 1)
        wq_ref[...] = jnp.where(lane // HEAD_DIM == (row % 8) // 4, qh, 0).astype(BF16)
        qrep = jnp.concatenate([q8] * N_HEADS, axis=0)
        row = lax.broadcasted_iota(jnp.int32, qrep.shape, 0)
        col = lax.broadcasted_iota(jnp.int32, qrep.shape, 1)
        wn_ref[...] = jnp.where(col // HEAD_DIM == row // 4, qrep, 0).astype(BF16)
        row = lax.broadcasted_iota(jnp.int32, mask_ref.shape, 0)
        col = lax.broadcasted_iota(jnp.int32, mask_ref.shape, 1)
        mask_ref[...] = jnp.where(col % N_HEADS == row // 8, 0.0, NEG_INF)
        m_ref[...] = jnp.full(m_ref.shape, NEG_INF, F32)
        l_ref[...] = jnp.zeros(l_ref.shape, F32)
        acc_ref[...] = jnp.zeros(acc_ref.shape, F32)

    def update(s, values, own_head_blocks):
        m_old = m_ref[...]
        m_new = jnp.maximum(m_old, jnp.max(s, axis=-1, keepdims=True))
        p = jnp.exp2(s - m_new)
        corr = jnp.exp2(m_old - m_new)
        m_ref[...] = m_new
        l_ref[...] = l_ref[...] * corr + jnp.sum(p, axis=-1, keepdims=True)
        pb = p.astype(BF16)
        pv, off = None, 0
        for val in values:
            n = val.shape[0]
            part = jnp.dot(pb[:, off:off + n], val, preferred_element_type=F32)
            pv = part if pv is None else pv + part
            off += n
        if own_head_blocks:
            pv = jnp.concatenate(
                [pv[h * 8:(h + 1) * 8, h * V_DIM:(h + 1) * V_DIM] for h in range(N_HEADS)], axis=0)
        acc_ref[...] = acc_ref[...] * corr + pv

    def pages_step():
        wq = wq_ref[...]
        group = min(pages, 16)
        for g0 in range(0, pages, group):
            parts = []
            for j in range(g0, g0 + group):
                sj = lax.dot_general(wq, k_refs[j][0].astype(BF16), nt, preferred_element_type=F32)
                sj = sj + mask_ref[...]
                if j == pages - 1:
                    sj = sj + jnp.where(c == last, bias_ref[:, :page_rows], 0.0)
                parts.append(sj)
            update(jnp.concatenate(parts, axis=-1),
                   [v_refs[j][0].astype(BF16) for j in range(g0, g0 + group)], False)

    def finish_body():
        pad = jnp.zeros((page - 8, width), F32)
        kn = jnp.concatenate([kn_ref[0], pad], axis=0).astype(BF16)
        vn = jnp.concatenate([vn_ref[0], pad], axis=0).astype(BF16)
        s_new = lax.dot_general(wn_ref[...], kn, nt, preferred_element_type=F32)
        update(s_new + bias_ref[:, page_rows:], [vn], True)
        lam = _lambda(lam_ref, lam_init)
        d = acc_ref[...] / l_ref[...]
        o = d - lam * pltpu.roll(d, _SROWS - 4, 0)
        o = _rms(o, g_ref[...]) * (1.0 - lam_init)
        for h in range(N_HEADS):
            o_ref[0, :, h * V_DIM:(h + 1) * V_DIM] = o[h * 8:(h + 1) * 8]

    return (lambda: pl.when(c == 0)(start_body)), pages_step, (lambda: pl.when(c == last)(finish_body))


def _attn_kernel(pt_ref, lam_ref, g_ref, q_ref, kb_ref, vt_ref, biasp_ref, qs_ref, kn_ref, vn_ref,
                 biass_ref, *rest, lam_init, pages, chunks):
    del pt_ref
    k_refs = rest[:pages]
    v_refs = rest[pages:2 * pages]
    o_ref, os_ref = rest[2 * pages:2 * pages + 2]
    s_ref, acc_ref, ml_ref, wq_ref, wn_ref, mask_ref, accs_ref, ms_ref, ls_ref = rest[2 * pages + 2:]
    step = (pl.program_id(0) * pl.num_programs(1) + pl.program_id(1)) * pl.num_programs(2) + pl.program_id(2)
    sample_start, sample_pages, sample_finish = _sample_attn_steps(
        step % chunks, chunks - 1, lam_ref, g_ref, qs_ref, kn_ref, vn_ref, biass_ref,
        k_refs, v_refs, os_ref, wq_ref, wn_ref, mask_ref, accs_ref, ms_ref, ls_ref, lam_init)
    sample_start()
    _prompt_attn_body(pl.program_id(2), lam_ref, g_ref, q_ref, kb_ref, vt_ref, biasp_ref, o_ref,
                      s_ref, acc_ref, ml_ref, lam_init, sample_pages)
    sample_finish()


def _attention(page_table, lam_rows, g_subln, q, kb, vt, bias_prompt, q8, kn8, vn8, bias_sample,
               cache_k, cache_v, lam_init):
    b, t, d = q.shape
    bd, n_pages = page_table.shape
    page_rows, vdim = cache_k.shape[1:]
    tile = min(ATTN_TILE, t)
    nq = t // tile
    assert t % tile == 0 and tile >= MAX_DISTANCE and bias_prompt.shape == (N_HEADS, 2, tile, tile)
    steps = N_HEADS * b * nq
    assert (bd * n_pages) % steps == 0
    pages = bd * n_pages // steps
    assert n_pages % pages == 0 and (pages <= 8 or pages % 8 == 0)
    chunks = n_pages // pages

    def seq_of(h, i, j):
        return ((h * b + i) * nq + j) // chunks

    def page_spec(p):
        return pl.BlockSpec(
            (1, page_rows, vdim), lambda h, i, j, pt: (pt[((h * b + i) * nq + j) * pages + p], 0, 0))

    small = pl.BlockSpec((1, 8, d), lambda h, i, j, pt: (seq_of(h, i, j), 0, 0))
    grid_spec = pltpu.PrefetchScalarGridSpec(
        num_scalar_prefetch=1,
        grid=(N_HEADS, b, nq),
        in_specs=[
            pl.BlockSpec((4, HEAD_DIM), lambda h, i, j, pt: (0, 0)),
            pl.BlockSpec((1, V_DIM), lambda h, i, j, pt: (0, 0)),
            pl.BlockSpec((1, tile, V_DIM), lambda h, i, j, pt: (i, j, h)),
            pl.BlockSpec((1, t, V_DIM), lambda h, i, j, pt: (i, 0, h)),
            pl.BlockSpec((1, V_DIM, t), lambda h, i, j, pt: (i, h, 0)),
            pl.BlockSpec((1, 2, tile, tile), lambda h, i, j, pt: (h, 0, 0, 0), pipeline_mode=pl.Buffered(1)),
            small, small, small,
            pl.BlockSpec(bias_sample.shape, lambda h, i, j, pt: (0, 0)),
        ] + [page_spec(p) for p in range(pages)] + [page_spec(p) for p in range(pages)],
        out_specs=[
            pl.BlockSpec((1, tile, V_DIM), lambda h, i, j, pt: (i, j, h)),
            pl.BlockSpec((1, 8, d), lambda h, i, j, pt: (seq_of(h, i, j), 0, 0)),
        ],
        scratch_shapes=[
            pltpu.VMEM((2, 2, tile, tile), F32),
            pltpu.VMEM((2, V_DIM, tile), F32),
            pltpu.VMEM((8, tile), F32),
            pltpu.VMEM((_SROWS, V_DIM), BF16),
            pltpu.VMEM((_SROWS, d), BF16),
            pltpu.VMEM((_SROWS, page_rows), F32),
            pltpu.VMEM((_SROWS, V_DIM), F32),
            pltpu.VMEM((_SROWS, 1), F32),
            pltpu.VMEM((_SROWS, 1), F32),
        ],
    )
    return pl.pallas_call(
        functools.partial(_attn_kernel, lam_init=lam_init, pages=pages, chunks=chunks),
        grid_spec=grid_spec,
        out_shape=[jax.ShapeDtypeStruct((b, t, d), BF16), jax.ShapeDtypeStruct((bd, 8, d), F32)],
        compiler_params=pltpu.CompilerParams(
            dimension_semantics=("arbitrary", "arbitrary", "arbitrary"), vmem_limit_bytes=VMEM_LIMIT),
        name="attention",
    )(page_table.reshape(-1), lam_rows, g_subln.reshape(1, V_DIM), q, kb, vt, bias_prompt, q8, kn8, vn8,
      bias_sample,
      *([cache_k] * pages), *([cache_v] * pages))


def _finish(d_groups, pg, o_n, ag, mgp, mga, x, gate, wg_ref, ps_ref, wpp_ref, wpa_ref, wo_ref, gp_ref):
    y = jnp.concatenate(
        [jnp.dot(d, wg_ref[g], preferred_element_type=F32) for g, d in enumerate(d_groups)], axis=-1)
    y = (y * ps_ref[...] * pg.astype(F32)).astype(BF16)
    y_pool = jnp.dot(y, wpp_ref[...], preferred_element_type=F32)
    a = (o_n.astype(F32) * ag.astype(F32)).astype(BF16)
    y_attn = jnp.dot(a, wpa_ref[...], preferred_element_type=F32)
    merged = (mgp.astype(F32) * y_pool + mga.astype(F32) * y_attn).astype(BF16)
    r = jnp.dot(merged, wo_ref[...], preferred_element_type=F32)
    return x + gate * _rms(r, gp_ref[...])


def _prompt_post_kernel(u_ref, halo_ref, pg_ref, o_ref, ag_ref, mgp_ref, mga_ref, x_ref, gate_ref,
                        wg_ref, ps_ref, wpp_ref, wpa_ref, wo_ref, gp_ref, y_ref, ext_ref, sum_a_ref, sum_b_ref):
    i = pl.program_id(1)
    rows = u_ref.shape[1]
    gdim = wg_ref.shape[1]
    d = u_ref.shape[2]
    total = HALO_ROWS + rows
    u = u_ref[0]
    ext_ref[0:HALO_ROWS, :] = jnp.where(i == 0, 0.0, halo_ref[0])
    ext_ref[HALO_ROWS:, :] = u

    def stage(src_ref, dst_ref, shift, first_row, first_col):
        n = total - first_row
        dst_ref[first_row:, first_col:] = (
            src_ref[first_row:, first_col:] + src_ref[first_row - shift:first_row - shift + n, first_col:])

    assert POOL_WINDOWS == (2, 4, 8, 16) and HALO_ROWS == 32 and d == 4 * gdim
    stage(ext_ref, sum_a_ref, 1, 8, 0)
    stage(sum_a_ref, sum_b_ref, 2, 16, gdim)
    stage(sum_b_ref, sum_a_ref, 4, 24, 2 * gdim)
    wide = (sum_a_ref[HALO_ROWS:, 3 * gdim:] + sum_a_ref[HALO_ROWS - 8:total - 8, 3 * gdim:])
    sums = (sum_a_ref[HALO_ROWS:, 0:gdim], sum_b_ref[HALO_ROWS:, gdim:2 * gdim],
            sum_a_ref[HALO_ROWS:, 2 * gdim:3 * gdim], wide)

    pos = i * rows + lax.broadcasted_iota(jnp.int32, (rows, 1), 0)
    d_groups = []
    for g, w in enumerate(POOL_WINDOWS):
        inv_cnt = 1.0 / jnp.minimum(pos + 1, w).astype(F32)
        d_groups.append((sums[g] * inv_cnt - u[:, g * gdim:(g + 1) * gdim]).astype(BF16))
    y_ref[0] = _finish(d_groups, pg_ref[0], o_ref[0], ag_ref[0], mgp_ref[0], mga_ref[0], x_ref[0],
                       gate_ref[0], wg_ref, ps_ref, wpp_ref, wpa_ref, wo_ref, gp_ref)


def _sample_post_kernel(hist_ref, u_ref, pg_ref, o_ref, ag_ref, mgp_ref, mga_ref, x_ref, gate_ref,
                        wg_ref, ps_ref, wpp_ref, wpa_ref, wo_ref, gp_ref, y_ref, *, start_pos):
    n_tok = u_ref.shape[0]
    gdim = wg_ref.shape[1]
    ext = [hist_ref[j] for j in range(POOL_BUF)] + [u_ref[t] for t in range(n_tok)]
    d_groups = []
    for g, w in enumerate(POOL_WINDOWS):
        cols = slice(g * gdim, (g + 1) * gdim)
        per_tok = []
        for t in range(n_tok):
            s = ext[POOL_BUF + t][:, cols]
            for back in range(1, w):
                s = s + ext[POOL_BUF + t - back][:, cols]
            cnt = float(min(start_pos + t + 1, w))
            per_tok.append((s / cnt - ext[POOL_BUF + t][:, cols]).astype(BF16))
        d_groups.append(jnp.concatenate(per_tok, axis=0))
    y_ref[...] = _finish(d_groups, pg_ref[...], o_ref[...], ag_ref[...], mgp_ref[...], mga_ref[...],
                         x_ref[...], gate_ref[...], wg_ref, ps_ref, wpp_ref, wpa_ref, wo_ref, gp_ref)


def _weight_specs(d, gdim, index):
    return [
        pl.BlockSpec((len(POOL_WINDOWS), gdim, gdim), lambda *a: (0, 0, 0)),
        pl.BlockSpec((1, d), lambda *a: (0, 0)),
        pl.BlockSpec((d, d), lambda *a: (0, 0)),
        pl.BlockSpec((d, d), lambda *a: (0, 0)),
        pl.BlockSpec((d, d), lambda *a: (0, 0)),
        pl.BlockSpec((1, d), lambda *a: (0, 0)),
    ]


def _prompt_post(u, pg, o_n, ag, mgp, mga, x, gate, weights):
    b, t, d = x.shape
    rows = min(POST_ROWS, t)
    gdim = d // len(POOL_WINDOWS)
    row_spec = pl.BlockSpec((1, rows, d), lambda i, j: (i, j, 0))
    halo_spec = pl.BlockSpec(
        (1, HALO_ROWS, d), lambda i, j: (i, jnp.maximum(j * (rows // HALO_ROWS) - 1, 0), 0))
    return pl.pallas_call(
        _prompt_post_kernel,
        grid=(b, t // rows),
        in_specs=[row_spec, halo_spec] + [row_spec] * 6
        + [pl.BlockSpec((1, 1, d), lambda i, j: (i, 0, 0))] + _weight_specs(d, gdim, None),
        out_specs=row_spec,
        out_shape=jax.ShapeDtypeStruct((b, t, d), F32),
        scratch_shapes=[pltpu.VMEM((HALO_ROWS + rows, d), F32)] * 3,
        compiler_params=pltpu.CompilerParams(
            dimension_semantics=("parallel", "parallel"), vmem_limit_bytes=VMEM_LIMIT),
        name="prompt_post",
    )(u, u, pg, o_n, ag, mgp, mga, x, gate, *weights)


def _sample_post(hist, u, pg, o_n, ag, mgp, mga, x, gate, weights, start_pos):
    n_tok, bd, d = u.shape
    n = n_tok * bd
    gdim = d // len(POOL_WINDOWS)
    flat = pl.BlockSpec((n, d), lambda i: (0, 0))
    return pl.pallas_call(
        functools.partial(_sample_post_kernel, start_pos=start_pos),
        grid=(1,),
        in_specs=[pl.BlockSpec(hist.shape, lambda i: (0, 0, 0)), pl.BlockSpec(u.shape, lambda i: (0, 0, 0))]
        + [flat] * 7 + _weight_specs(d, gdim, None),
        out_specs=flat,
        out_shape=jax.ShapeDtypeStruct((n, d), F32),
        compiler_params=pltpu.CompilerParams(vmem_limit_bytes=VMEM_LIMIT),
        name="sample_post",
    )(hist, u, pg, o_n, ag, mgp, mga, x, gate, *weights)


def _token_major(a):
    return jnp.swapaxes(a, 0, 1)


def kernel(x_prompt, x_sample, cache_k, cache_v, state_pool, page_table, c_prompt, c_sample, rel_bias, w_ada, b_ada, g_pre, g_post, w_in, w_pool_grp, pool_scale, w_proj_pool, lambda_q1, lambda_k1, lambda_q2, lambda_k2, g_subln, w_proj_attn, w_out):
    depth = w_in.shape[0]
    b, t, d = x_prompt.shape
    bd, ts, _ = x_sample.shape
    n_pages = page_table.shape[1]
    n_phys, page = cache_k.shape[1], cache_k.shape[2]
    past_len = n_pages * page
    assert ts == 4 and d == N_HEADS * V_DIM

    tile = min(ATTN_TILE, t)
    ar = jnp.arange(tile, dtype=jnp.int32)
    n_prompt = jnp.stack([ar[None, :] - ar[:, None], tile + ar[None, :] - ar[:, None]])
    bias_prompt = _bias_tiles(rel_bias, n_prompt.reshape(2 * tile, tile)).reshape(N_HEADS, 2, tile, tile)

    tq = jnp.arange(8, dtype=jnp.int32)[:, None] % ts
    tk = jnp.arange(page, dtype=jnp.int32)[None, :]
    n_new = jnp.where(tk < ts, tq - tk, -1)
    n_sample = jnp.concatenate([page + tq - tk, n_new], axis=1)
    bias_sample = _bias_tiles(rel_bias, n_sample).reshape(_SROWS, 2 * page)
    bias_sample = jnp.concatenate(
        [jnp.repeat(bias_sample[:, :page], N_HEADS, axis=1), bias_sample[:, page:]], axis=1)

    xp = x_prompt
    xs = _token_major(x_sample).reshape(1, ts * bd, d)
    outs = [[] for _ in range(6)]
    for l in range(depth):
        lam_init = 0.8 - 0.6 * math.exp(-0.3 * l)
        lam_rows = jnp.stack([lambda_q1[l], lambda_k1[l], lambda_q2[l], lambda_k2[l]])
        w_in_l = w_in[l].astype(BF16)
        weights = (w_pool_grp[l].astype(BF16), pool_scale[l].reshape(1, d), w_proj_pool[l].astype(BF16),
                   w_proj_attn[l].astype(BF16), w_out[l].astype(BF16), g_post[l].reshape(1, d))

        mod = _modulation(jnp.concatenate([c_prompt, c_sample]), w_ada[l], b_ada[l])
        shift, scale, gate = mod[:, :d], mod[:, d:2 * d], mod[:, 2 * d:]

        u, pg, q, k, v, ag, mgp, mga, kb, vt = _in_projection(
            xp, scale[:b, None], shift[:b, None], g_pre[l], w_in_l, min(IN_ROWS, t), True)
        tile_rows = lambda a: jnp.tile(a, (ts, 1))[None]
        us, pgs, qs, ks, vs, ags, mgps, mgas = _in_projection(
            xs, tile_rows(scale[b:]), tile_rows(shift[b:]), g_pre[l], w_in_l, ts * bd, False)

        seq_major = lambda a: _token_major(a.reshape(ts, bd, d))
        pad8 = lambda a: jnp.concatenate([a, jnp.zeros_like(a)], axis=1)
        q_s = seq_major(qs)
        o_n, o8 = _attention(
            page_table, lam_rows, g_subln[l], q, kb, vt, bias_prompt,
            jnp.concatenate([q_s, q_s], axis=1), pad8(seq_major(ks)), pad8(seq_major(vs)), bias_sample,
            cache_k[l].reshape(n_phys, page * N_HEADS, V_DIM),
            cache_v[l].reshape(n_phys, page * N_HEADS, V_DIM), lam_init)

        xp = _prompt_post(u, pg, o_n, ag, mgp, mga, xp, gate[:b, None], weights)
        outs[0].append(k.reshape(b, t, N_HEADS, V_DIM))
        outs[1].append(v.reshape(b, t, N_HEADS, V_DIM))
        outs[2].append(u[:, t - POOL_BUF:])

        o_ns = _token_major(o8[:, :ts]).reshape(ts * bd, d)
        hist = _token_major(state_pool[l])
        flat = lambda a: a.reshape(ts * bd, d)
        xs = _sample_post(hist, us.reshape(ts, bd, d), flat(pgs), o_ns, flat(ags), flat(mgps), flat(mgas),
                          flat(xs), jnp.tile(gate[b:], (ts, 1)), weights, past_len).reshape(1, ts * bd, d)
        outs[3].append(seq_major(ks).reshape(bd, ts, N_HEADS, V_DIM))
        outs[4].append(seq_major(vs).reshape(bd, ts, N_HEADS, V_DIM))
        outs[5].append(jnp.concatenate([state_pool[l], seq_major(us)], axis=1)[:, ts:])

    y_sample = _token_major(xs.reshape(ts, bd, d))
    return (xp, y_sample) + tuple(jnp.stack(o) for o in outs)
```
